```python
import math
import jax, jax.numpy as jnp
from jax import lax
import numpy as np

D_MODEL = 2048
BATCH = 4
SEQ = 4096
DEPTH = 4

N_EVEN = (DEPTH + 1) // 2
N_ODD = DEPTH // 2

A_GROUPS = 8
A_DIM = 128
A_CHUNK = 128
A_WIDTH = A_GROUPS * A_DIM

B_HEADS = 4
B_DK = 128
B_DV = 256
B_RANK = 16
B_TAU = 16.0
B_CHUNK = 64
B_WIDTH = B_HEADS * B_DV

AB_OUT = A_WIDTH + B_WIDTH
AB_SPLITS = (A_WIDTH, A_WIDTH, B_HEADS * B_DK, B_HEADS * B_DK, B_WIDTH, B_WIDTH, B_RANK)
AB_IN = 2 * A_WIDTH + 2 * B_HEADS * B_DK + 2 * B_WIDTH + B_RANK

C_HEADS = 16
C_KV_HEADS = 4
C_HEAD_DIM = 128
C_IDX_HEADS = 16
C_IDX_DIM = 64
C_TOPK_MAX = 256
C_QBLOCK = 128
C_WIDTH = C_HEADS * C_HEAD_DIM
C_SPLITS = (C_WIDTH, C_KV_HEADS * C_HEAD_DIM, C_KV_HEADS * C_HEAD_DIM, C_IDX_HEADS * C_IDX_DIM, C_IDX_DIM, C_IDX_HEADS)
C_IN = C_WIDTH + 2 * C_KV_HEADS * C_HEAD_DIM + C_IDX_HEADS * C_IDX_DIM + C_IDX_DIM + C_IDX_HEADS

REL_BUCKETS = 32
REL_MAX_DIST = 128

D_FF = 4 * D_MODEL
EPS = 1e-6
F32 = jnp.float32

kernel_name = 'hybrid_sgu_gla_dsa_trunk'


def _offsets(sizes):
    out, acc = [], 0
    for s in sizes[:-1]:
        acc += s
        out.append(acc)
    return out


def _rmsnorm(x, g):
    xf = x.astype(F32)
    y = xf * lax.rsqrt(jnp.mean(xf * xf, axis=-1, keepdims=True) + EPS)
    return (y * g.astype(F32)).astype(x.dtype)


def _layernorm(x, g):
    xf = x.astype(F32)
    mu = jnp.mean(xf, axis=-1, keepdims=True)
    xc = xf - mu
    var = jnp.mean(xc * xc, axis=-1, keepdims=True)
    return (xc * lax.rsqrt(var + EPS) * g.astype(F32)).astype(x.dtype)


def _t5_bucket(dist):
    max_exact = REL_BUCKETS // 2
    d = jnp.maximum(dist, 1).astype(F32)
    large = max_exact + (jnp.log(d / max_exact) / math.log(REL_MAX_DIST / max_exact)
                         * (REL_BUCKETS - max_exact)).astype(jnp.int32)
    large = jnp.minimum(large, REL_BUCKETS - 1)
    return jnp.where(dist < max_exact, dist, large)


def _spatial_gating(u, v, ln_g, w_s, b_s):
    bsz, L = u.shape[:2]
    nc = L // A_CHUNK
    v = _layernorm(v, ln_g)
    v = v.reshape(bsz, nc, A_CHUNK, A_GROUPS, A_DIM)
    causal = jnp.tril(jnp.ones((A_CHUNK, A_CHUNK), dtype=bool))
    w = jnp.where(causal[None], w_s, jnp.zeros_like(w_s))
    z = jnp.einsum('gts,bnsgc->bntgc', w.astype(v.dtype), v) + b_s.T[:, :, None].astype(v.dtype)
    return u * z.reshape(bsz, L, A_GROUPS, A_DIM)


def _gla(q, k, v, log_a):
    bsz, L = q.shape[:2]
    nc = L // B_CHUNK

    def to_chunks(t):
        return t.astype(F32).reshape(bsz, nc, B_CHUNK, B_HEADS, t.shape[-1]).transpose(1, 0, 3, 2, 4)

    qc = to_chunks(q) * (B_DK ** -0.5)
    kc = to_chunks(k)
    vc = to_chunks(v)
    cum = jnp.cumsum(to_chunks(log_a), axis=3)
    last = cum[:, :, :, -1:, :]
    ref = 0.5 * last
    qe = qc * jnp.exp(cum - ref)
    ke = kc * jnp.exp(ref - cum)
    causal = jnp.tril(jnp.ones((B_CHUNK, B_CHUNK), dtype=bool))
    scores = jnp.where(causal, jnp.einsum('nbhid,nbhjd->nbhij', qe, ke), 0.0)
    o_intra = jnp.einsum('nbhij,nbhjv->nbhiv', scores, vc)
    q_inter = qc * jnp.exp(cum)
    k_state = kc * jnp.exp(last - cum)
    decay = jnp.exp(last[:, :, :, 0, :])

    def step(S, inp):
        qi, ki, vi, di = inp
        o = jnp.einsum('bhid,bhdv->bhiv', qi, S)
        S = di[..., None] * S + jnp.einsum('bhjd,bhjv->bhdv', ki, vi)
        return S, o

    S0 = jnp.zeros((bsz, B_HEADS, B_DK, B_DV), F32)
    _, o_inter = lax.scan(step, S0, (q_inter, k_state, vc, decay))
    o = o_intra + o_inter
    return o.transpose(1, 0, 3, 2, 4).reshape(bsz, L, B_HEADS, B_DV)


def _dsa_attention(q, k, v, iq, ik, iw, rel_bias):
    bsz, L = q.shape[:2]
    k_sel = min(C_TOPK_MAX, L // 4)
    nb = L // C_QBLOCK
    grp = C_HEADS // C_KV_HEADS
    key_pos = jnp.arange(L, dtype=jnp.int32)
    ik32 = ik.astype(F32)

    def blocks(t):
        return t.reshape(bsz, nb, C_QBLOCK, *t.shape[2:]).swapaxes(0, 1)

    def one_block(inp):
        qb, iqb, iwb, start = inp
        qpos = start + jnp.arange(C_QBLOCK, dtype=jnp.int32)
        s = jnp.einsum('bthd,bsd->bths', iqb.astype(F32), ik32) * (C_IDX_DIM ** -0.5)
        score = jnp.einsum('bths,bth->bts', jax.nn.relu(s), iwb.astype(F32)) * (C_IDX_HEADS ** -0.5)
        causal = key_pos[None, :] <= qpos[:, None]
        score = jnp.where(causal[None], score, -jnp.inf)
        _, idx = lax.top_k(score, k_sel)
        valid = idx <= qpos[None, :, None]
        kg = jax.vmap(lambda kb, ib: kb[ib])(k, idx)
        vg = jax.vmap(lambda vb, ib: vb[ib])(v, idx)
        qg = qb.reshape(bsz, C_QBLOCK, C_KV_HEADS, grp, C_HEAD_DIM)
        logits = jnp.einsum('btkgd,btskd->btkgs', qg, kg).astype(F32) * (C_HEAD_DIM ** -0.5)
        dist = jnp.maximum(qpos[None, :, None] - idx, 0)
        bias = rel_bias.astype(F32)[_t5_bucket(dist)]
        bias = bias.reshape(bsz, C_QBLOCK, k_sel, C_KV_HEADS, grp).transpose(0, 1, 3, 4, 2)
        logits = jnp.where(valid[:, :, None, None, :], logits + bias, -jnp.inf)
        p = jax.nn.softmax(logits, axis=-1)
        o = jnp.einsum('btkgs,btskd->btkgd', p.astype(vg.dtype), vg)
        return o.reshape(bsz, C_QBLOCK, C_WIDTH)

    starts = jnp.arange(nb, dtype=jnp.int32) * C_QBLOCK
    out = lax.map(one_block, (blocks(q), blocks(iq), blocks(iw), starts))
    return out.swapaxes(0, 1).reshape(bsz, L, C_WIDTH)


def _even_mixer(h, w_in, v_ln_g, w_s, b_s, gate_w2, gate_b, out_norm_g, w_out):
    bsz, L, _ = h.shape
    p = h @ w_in
    a_u, a_v, q, k, v, r, g_lr = jnp.split(p, _offsets(AB_SPLITS), axis=-1)
    a_u = jax.nn.gelu(a_u, approximate=False).reshape(bsz, L, A_GROUPS, A_DIM)
    a_v = jax.nn.gelu(a_v, approximate=False).reshape(bsz, L, A_GROUPS, A_DIM)
    a_out = _spatial_gating(a_u, a_v, v_ln_g, w_s, b_s).reshape(bsz, L, A_WIDTH)
    log_a = jax.nn.log_sigmoid((g_lr @ gate_w2 + gate_b).astype(F32)) / B_TAU
    o = _gla(q.reshape(bsz, L, B_HEADS, B_DK), k.reshape(bsz, L, B_HEADS, B_DK),
             v.reshape(bsz, L, B_HEADS, B_DV), log_a.reshape(bsz, L, B_HEADS, B_DK))
    o = _rmsnorm(o, out_norm_g) * jax.nn.silu(r.astype(F32)).reshape(bsz, L, B_HEADS, B_DV)
    b_out = o.reshape(bsz, L, B_WIDTH).astype(h.dtype)
    return jnp.concatenate([a_out, b_out], axis=-1) @ w_out


def _odd_mixer(h, w_in, w_out, rel_bias):
    bsz, L, _ = h.shape
    p = h @ w_in
    q, k, v, iq, ik, iw = jnp.split(p, _offsets(C_SPLITS), axis=-1)
    o = _dsa_attention(q.reshape(bsz, L, C_HEADS, C_HEAD_DIM),
                       k.reshape(bsz, L, C_KV_HEADS, C_HEAD_DIM),
                       v.reshape(bsz, L, C_KV_HEADS, C_HEAD_DIM),
                       iq.reshape(bsz, L, C_IDX_HEADS, C_IDX_DIM), ik, iw, rel_bias)
    return o @ w_out


def _sqrelu_mlp(h, w1, w2):
    return jnp.square(jax.nn.relu(h @ w1)) @ w2


def setup_inputs(seed: int = 0) -> dict:
    key = jax.random.key(seed)
    ks = jax.random.split(key, 17)

    def nrm(k, shape, scale):
        return jax.random.normal(k, shape, F32) * scale

    row_scale = (jnp.arange(A_CHUNK, dtype=F32) + 1.0) ** -0.5
    return {
        'x': nrm(ks[0], (BATCH, SEQ, D_MODEL), 1.0),
        'norm_mix_g': 1.0 + nrm(ks[1], (DEPTH, D_MODEL), 0.02),
        'norm_ffn_g': 1.0 + nrm(ks[2], (DEPTH, D_MODEL), 0.02),
        'final_norm_g': 1.0 + nrm(ks[3], (D_MODEL,), 0.02),
        'ab_w_in': nrm(ks[4], (N_EVEN, D_MODEL, AB_IN), D_MODEL ** -0.5),
        'a_v_ln_g': 1.0 + nrm(ks[5], (N_EVEN, A_GROUPS, A_DIM), 0.02),
        'a_w_s': nrm(ks[6], (N_EVEN, A_GROUPS, A_CHUNK, A_CHUNK), 1.0) * row_scale[:, None],
        'a_b_s': nrm(ks[7], (N_EVEN, A_GROUPS, A_CHUNK), 0.02),
        'b_gate_w2': nrm(ks[8], (N_EVEN, B_RANK, B_HEADS * B_DK), B_RANK ** -0.5),
        'b_gate_b': nrm(ks[9], (N_EVEN, B_HEADS * B_DK), 0.02),
        'b_out_norm_g': 1.0 + nrm(ks[10], (N_EVEN, B_DV), 0.02),
        'ab_w_out': nrm(ks[11], (N_EVEN, AB_OUT, D_MODEL), AB_OUT ** -0.5),
        'c_w_in': nrm(ks[12], (N_ODD, D_MODEL, C_IN), D_MODEL ** -0.5),
        'c_w_out': nrm(ks[13], (N_ODD, C_WIDTH, D_MODEL), C_WIDTH ** -0.5),
        'rel_bias': nrm(ks[14], (REL_BUCKETS, C_HEADS), 0.2),
        'ffn_w1': nrm(ks[15], (DEPTH, D_MODEL, D_FF), D_MODEL ** -0.5),
        'ffn_w2': nrm(ks[16], (DEPTH, D_FF, D_MODEL), D_FF ** -0.5),
    }


def reference(x, norm_mix_g, norm_ffn_g, final_norm_g, ab_w_in, a_v_ln_g, a_w_s, a_b_s,
              b_gate_w2, b_gate_b, b_out_norm_g, ab_w_out, c_w_in, c_w_out, rel_bias,
              ffn_w1, ffn_w2):
    h = x
    for layer in range(DEPTH):
        i = layer // 2
        hn = _rmsnorm(h, norm_mix_g[layer])
        if layer % 2 == 0:
            mix = _even_mixer(hn, ab_w_in[i], a_v_ln_g[i], a_w_s[i], a_b_s[i], b_gate_w2[i],
                              b_gate_b[i], b_out_norm_g[i], ab_w_out[i])
        else:
            mix = _odd_mixer(hn, c_w_in[i], c_w_out[i], rel_bias)
        h = h + mix.astype(h.dtype)
        hn = _rmsnorm(h, norm_ffn_g[layer])
        h = h + _sqrelu_mlp(hn, ffn_w1[layer], ffn_w2[layer]).astype(h.dtype)
    return _rmsnorm(h, final_norm_g)
```

```python
import functools
import math

import jax
import jax.numpy as jnp
from jax import lax
from jax.experimental import pallas as pl
from jax.experimental.pallas import tpu as pltpu

F32 = jnp.float32
BF16 = jnp.bfloat16
EPS = 1e-6

VMEM_LIMIT_BYTES = 56 * 1024 * 1024
LANES = 128

A_GROUPS = 8
A_DIM = 128
A_CHUNK = 128
A_WIDTH = A_GROUPS * A_DIM
B_HEADS = 4
B_DK = 128
B_DV = 256
B_RANK = 16
B_TAU = 16.0
B_CHUNK = 64
B_WIDTH = B_HEADS * B_DV
C_HEADS = 16
C_KV_HEADS = 4
C_GROUP = C_HEADS // C_KV_HEADS
C_HEAD_DIM = 128
C_IDX_HEADS = 16
C_IDX_DIM = 64
C_TOPK_MAX = 256
C_QBLOCK = 128
C_WIDTH = C_HEADS * C_HEAD_DIM
C_KVW = C_KV_HEADS * C_HEAD_DIM
REL_BUCKETS = 32
REL_MAX_DIST = 128
MASK_NEG = -1e30


def _params(*sem):
    return pltpu.CompilerParams(dimension_semantics=sem, vmem_limit_bytes=VMEM_LIMIT_BYTES)


def _rms_rows(x, g):
    ms = jnp.mean(x * x, axis=-1, keepdims=True)
    return x * lax.rsqrt(ms + EPS) * g


def _norm_matmul_kernel(x_ref, g_ref, w_ref, we_ref, o_ref, oe_ref, xn_ref):
    @pl.when(pl.program_id(1) == 0)
    def _():
        xn = _rms_rows(x_ref[...], g_ref[...]).astype(BF16)
        xn_ref[...] = xn
        oe_ref[...] = jnp.dot(xn, we_ref[...], preferred_element_type=F32)

    o_ref[...] = jnp.dot(xn_ref[...], w_ref[...], preferred_element_type=F32).astype(o_ref.dtype)


def _norm_matmul(x, g, w_main, w_extra, out_dtype, tm=512, tn=1024):
    n, d = x.shape
    nm = w_main.shape[1]
    return pl.pallas_call(
        _norm_matmul_kernel,
        grid=(n // tm, nm // tn),
        in_specs=[
            pl.BlockSpec((tm, d), lambda i, j: (i, 0)),
            pl.BlockSpec((1, d), lambda i, j: (0, 0)),
            pl.BlockSpec((d, tn), lambda i, j: (0, j)),
            pl.BlockSpec((d, LANES), lambda i, j: (0, 0)),
        ],
        out_specs=[
            pl.BlockSpec((tm, tn), lambda i, j: (i, j)),
            pl.BlockSpec((tm, LANES), lambda i, j: (i, 0)),
        ],
        out_shape=[jax.ShapeDtypeStruct((n, nm), out_dtype), jax.ShapeDtypeStruct((n, LANES), F32)],
        scratch_shapes=[pltpu.VMEM((tm, d), BF16)],
        compiler_params=_params("parallel", "arbitrary"),
        name="norm_matmul",
    )(x, g.reshape(1, d), w_main, w_extra)


def _proj_res_kernel(*refs, n_in):
    h_ref = refs[0]
    o_ref = refs[1 + 2 * n_in]
    acc = h_ref[...]
    for k in range(n_in):
        acc = acc + jnp.dot(refs[1 + k][...], refs[1 + n_in + k][...], preferred_element_type=F32)
    o_ref[...] = acc


def _proj_residual(h, xs, ws, tm=512, tn=1024):
    n, d = h.shape
    n_in = len(xs)
    in_specs = [pl.BlockSpec((tm, tn), lambda i, j: (i, j))]
    in_specs += [pl.BlockSpec((tm, x.shape[1]), lambda i, j: (i, 0)) for x in xs]
    in_specs += [pl.BlockSpec((w.shape[0], tn), lambda i, j: (0, j)) for w in ws]
    return pl.pallas_call(
        functools.partial(_proj_res_kernel, n_in=n_in),
        grid=(n // tm, d // tn),
        in_specs=in_specs,
        out_specs=pl.BlockSpec((tm, tn), lambda i, j: (i, j)),
        out_shape=jax.ShapeDtypeStruct((n, d), F32),
        compiler_params=_params("parallel", "parallel"),
        name="proj_residual",
    )(h, *xs, *ws)


def _ffn_kernel(x_ref, g_ref, w1_ref, w2_ref, fg_ref, o_ref, xn_ref, acc_ref, *, final_norm):
    f = pl.program_id(1)

    @pl.when(f == 0)
    def _():
        xn_ref[...] = _rms_rows(x_ref[...], g_ref[...]).astype(BF16)
        acc_ref[...] = jnp.zeros_like(acc_ref)

    h1 = jnp.dot(xn_ref[...], w1_ref[...], preferred_element_type=F32)
    h1 = jnp.square(jnp.maximum(h1, 0.0)).astype(BF16)
    acc_ref[...] += jnp.dot(h1, w2_ref[...], preferred_element_type=F32)

    @pl.when(f == pl.num_programs(1) - 1)
    def _():
        y = x_ref[...] + acc_ref[...]
        if final_norm:
            y = _rms_rows(y, fg_ref[...])
        o_ref[...] = y


def _ffn(x, g, w1, w2, final_g, final_norm, tm=512, tf=1024):
    n, d = x.shape
    dff = w1.shape[1]
    return pl.pallas_call(
        functools.partial(_ffn_kernel, final_norm=final_norm),
        grid=(n // tm, dff // tf),
        in_specs=[
            pl.BlockSpec((tm, d), lambda i, f: (i, 0)),
            pl.BlockSpec((1, d), lambda i, f: (0, 0)),
            pl.BlockSpec((d, tf), lambda i, f: (0, f)),
            pl.BlockSpec((tf, d), lambda i, f: (f, 0)),
            pl.BlockSpec((1, d), lambda i, f: (0, 0)),
        ],
        out_specs=pl.BlockSpec((tm, d), lambda i, f: (i, 0)),
        out_shape=jax.ShapeDtypeStruct((n, d), F32),
        scratch_shapes=[pltpu.VMEM((tm, d), BF16), pltpu.VMEM((tm, d), F32)],
        compiler_params=_params("parallel", "arbitrary"),
        name="ffn",
    )(x, g.reshape(1, d), w1, w2, final_g.reshape(1, d))


def _gelu(x):
    return 0.5 * x * (1.0 + lax.erf(x * math.sqrt(0.5)))


def _sgu_kernel(u_ref, v_ref, lng_ref, w_ref, bt_ref, o_ref):
    t_i = lax.broadcasted_iota(jnp.int32, (A_CHUNK, A_CHUNK), 0)
    s_i = lax.broadcasted_iota(jnp.int32, (A_CHUNK, A_CHUNK), 1)
    causal = s_i <= t_i
    for g in range(A_GROUPS):
        sl = slice(g * A_DIM, (g + 1) * A_DIM)
        v = _gelu(v_ref[:, sl])
        mu = jnp.mean(v, axis=-1, keepdims=True)
        vc = v - mu
        var = jnp.mean(vc * vc, axis=-1, keepdims=True)
        vn = vc * lax.rsqrt(var + EPS) * lng_ref[g:g + 1, :]
        w = jnp.where(causal, w_ref[g], 0.0).astype(BF16)
        z = jnp.dot(w, vn.astype(BF16), preferred_element_type=F32) + bt_ref[:, g:g + 1]
        o_ref[:, sl] = (_gelu(u_ref[:, sl]) * z).astype(o_ref.dtype)


def _sgu(p_main, ln_g, w_s, b_s_t):
    n = p_main.shape[0]
    return pl.pallas_call(
        _sgu_kernel,
        grid=(n // A_CHUNK,),
        in_specs=[
            pl.BlockSpec((A_CHUNK, A_WIDTH), lambda i: (i, 0)),
            pl.BlockSpec((A_CHUNK, A_WIDTH), lambda i: (i, 1)),
            pl.BlockSpec((A_GROUPS, A_DIM), lambda i: (0, 0)),
            pl.BlockSpec((A_GROUPS, A_CHUNK, A_CHUNK), lambda i: (0, 0, 0)),
            pl.BlockSpec((A_CHUNK, A_GROUPS), lambda i: (0, 0)),
        ],
        out_specs=pl.BlockSpec((A_CHUNK, A_WIDTH), lambda i: (i, 0)),
        out_shape=jax.ShapeDtypeStruct((n, A_WIDTH), BF16),
        compiler_params=_params("parallel"),
        name="sgu",
    )(p_main, p_main, ln_g, w_s, b_s_t)


def _cumsum_rows(x):
    n = x.shape[0]
    row = lax.broadcasted_iota(jnp.int32, x.shape, 0)
    sh = 1
    while sh < n:
        x = x + jnp.where(row >= sh, pltpu.roll(x, sh, axis=0), 0.0)
        sh *= 2
    return x


def _log_sigmoid(x):
    return jnp.minimum(x, 0.0) - jnp.log1p(jnp.exp(-jnp.abs(x)))


def _gla_kernel(q_ref, k_ref, v_ref, r_ref, g_ref, w2_ref, gb_ref, ng_ref, o_ref, s_ref):
    @pl.when(pl.program_id(1) == 0)
    def _():
        s_ref[...] = jnp.zeros_like(s_ref)

    c = B_CHUNK
    gate = jnp.dot(g_ref[...].astype(BF16), w2_ref[...], preferred_element_type=F32) + gb_ref[...]
    log_a = _log_sigmoid(gate) * (1.0 / B_TAU)
    i_i = lax.broadcasted_iota(jnp.int32, (c, c), 0)
    j_i = lax.broadcasted_iota(jnp.int32, (c, c), 1)
    causal = j_i <= i_i
    for h in range(B_HEADS):
        ksl = slice(h * B_DK, (h + 1) * B_DK)
        vsl = slice(h * B_DV, (h + 1) * B_DV)
        cum = _cumsum_rows(log_a[:, ksl])
        last = cum[c - 1:c, :]
        ref = 0.5 * last
        q = q_ref[:, ksl] * (B_DK ** -0.5)
        k = k_ref[:, ksl]
        v = v_ref[:, vsl].astype(BF16)
        qe = (q * jnp.exp(cum - ref)).astype(BF16)
        ke = (k * jnp.exp(ref - cum)).astype(BF16)
        scores = lax.dot_general(qe, ke, (((1,), (1,)), ((), ())), preferred_element_type=F32)
        scores = jnp.where(causal, scores, 0.0).astype(BF16)
        o = jnp.dot(scores, v, preferred_element_type=F32)
        q_inter = (q * jnp.exp(cum)).astype(BF16)
        state = s_ref[h]
        o = o + jnp.dot(q_inter, state.astype(BF16), preferred_element_type=F32)
        cum_t = cum.T
        last_t = cum_t[:, c - 1:c]
        k_state_t = (k.T * jnp.exp(last_t - cum_t)).astype(BF16)
        s_ref[h] = jnp.exp(last_t) * state + jnp.dot(k_state_t, v, preferred_element_type=F32)
        o = _rms_rows(o, ng_ref[...])
        r = r_ref[:, vsl]
        o_ref[:, vsl] = (o * (r * jax.nn.sigmoid(r))).astype(o_ref.dtype)


def _gla(p_main, p_extra, gate_w2_pad, gate_b, norm_g, bsz, seq):
    n = p_main.shape[0]
    nc = seq // B_CHUNK
    c = B_CHUNK
    qk_w = B_HEADS * B_DK
    row = lambda b, i: b * nc + i
    return pl.pallas_call(
        _gla_kernel,
        grid=(bsz, nc),
        in_specs=[
            pl.BlockSpec((c, qk_w), lambda b, i: (row(b, i), (2 * A_WIDTH) // qk_w)),
            pl.BlockSpec((c, qk_w), lambda b, i: (row(b, i), (2 * A_WIDTH) // qk_w + 1)),
            pl.BlockSpec((c, B_WIDTH), lambda b, i: (row(b, i), (2 * A_WIDTH + 2 * qk_w) // B_WIDTH)),
            pl.BlockSpec((c, B_WIDTH), lambda b, i: (row(b, i), (2 * A_WIDTH + 2 * qk_w) // B_WIDTH + 1)),
            pl.BlockSpec((c, LANES), lambda b, i: (row(b, i), 0)),
            pl.BlockSpec((LANES, qk_w), lambda b, i: (0, 0)),
            pl.BlockSpec((1, qk_w), lambda b, i: (0, 0)),
            pl.BlockSpec((1, B_DV), lambda b, i: (0, 0)),
        ],
        out_specs=pl.BlockSpec((c, B_WIDTH), lambda b, i: (row(b, i), 0)),
        out_shape=jax.ShapeDtypeStruct((n, B_WIDTH), BF16),
        scratch_shapes=[pltpu.VMEM((B_HEADS, B_DK, B_DV), F32)],
        compiler_params=_params("parallel", "arbitrary"),
        name="gla",
    )(p_main, p_main, p_main, p_main, p_extra, gate_w2_pad, gate_b.reshape(1, qk_w),
      norm_g.reshape(1, B_DV))


def _t5_bucket(dist):
    max_exact = REL_BUCKETS // 2
    d = jnp.maximum(dist, 1).astype(F32)
    large = max_exact + (jnp.log(d / max_exact) / math.log(REL_MAX_DIST / max_exact)
                         * (REL_BUCKETS - max_exact)).astype(jnp.int32)
    large = jnp.minimum(large, REL_BUCKETS - 1)
    return jnp.where(dist < max_exact, dist, large)


def _bias_kernel(rb_ref, o_ref):
    s_i = lax.broadcasted_iota(jnp.int32, (C_QBLOCK, C_QBLOCK), 0)
    t_i = lax.broadcasted_iota(jnp.int32, (C_QBLOCK, C_QBLOCK), 1)
    for delta in range(2):
        bucket = _t5_bucket(jnp.maximum(delta * C_QBLOCK + t_i - s_i, 0))
        for h in range(C_HEADS):
            acc = jnp.zeros((C_QBLOCK, C_QBLOCK), F32)
            for b in range(REL_BUCKETS):
                acc = jnp.where(bucket == b, rb_ref[b, h], acc)
            o_ref[h, delta] = acc


def _bias_tiles(rel_bias):
    return pl.pallas_call(
        _bias_kernel,
        in_specs=[pl.BlockSpec(memory_space=pltpu.SMEM)],
        out_specs=pl.BlockSpec(memory_space=pltpu.VMEM),
        out_shape=jax.ShapeDtypeStruct((C_HEADS, 2, C_QBLOCK, C_QBLOCK), F32),
        name="bias_tiles",
    )(rel_bias)


def _dsa_kernel(rb_ref, q_ref, kv_ref, iq_ref, ex_ref, bias_ref, o_ref,
                keys_ref, mask_ref, vt_ref, iklo_ref, ikhi_ref, iqa_ref, qg_ref,
                acc_ref, m_ref, l_ref, *, seq, ksel):
    qb = pl.program_id(1)
    nb = seq // C_QBLOCK
    blk = C_QBLOCK
    n_pair = C_IDX_HEADS // 2
    s_io = lax.broadcasted_iota(jnp.int32, (blk, blk), 0)
    t_io = lax.broadcasted_iota(jnp.int32, (blk, blk), 1)
    lane = lax.broadcasted_iota(jnp.int32, (blk, LANES), 1)

    @pl.when(qb == 0)
    def _():
        def fill(j, carry):
            rows = pl.ds(pl.multiple_of(j * blk, blk), blk)
            ex = ex_ref[rows, :]
            iklo_ref[j] = jnp.where(lane < C_IDX_DIM, ex, 0.0).astype(BF16)
            ikhi_ref[j] = jnp.where(lane >= C_IDX_DIM, pltpu.roll(ex, C_IDX_DIM, axis=1), 0.0).astype(BF16)
            for kvh in range(C_KV_HEADS):
                vblk = kv_ref[rows, C_KVW + kvh * C_HEAD_DIM:C_KVW + (kvh + 1) * C_HEAD_DIM]
                vt_ref[kvh, j] = vblk.astype(F32).T.astype(BF16)
            return carry
        lax.fori_loop(0, nb, fill, 0)

    q_rows = pl.ds(pl.multiple_of(qb * blk, blk), blk)
    ext_t = ex_ref[q_rows, :].T
    idx_scale = (C_IDX_DIM ** -0.5) * (C_IDX_HEADS ** -0.5)
    for p in range(n_pair):
        iqa_ref[p * blk:(p + 1) * blk, :] = iq_ref[:, p * LANES:(p + 1) * LANES]
    for kvh in range(C_KV_HEADS):
        for g in range(C_GROUP):
            h = kvh * C_GROUP + g
            qg_ref[kvh, g * blk:(g + 1) * blk, :] = q_ref[:, h * C_HEAD_DIM:(h + 1) * C_HEAD_DIM]

    def idx_body(j, carry):
        mx, mn = carry
        iqa = iqa_ref[...]
        sc_even = lax.dot_general(iklo_ref[j], iqa, (((1,), (1,)), ((), ())), preferred_element_type=F32)
        sc_odd = lax.dot_general(ikhi_ref[j], iqa, (((1,), (1,)), ((), ())), preferred_element_type=F32)
        acc = jnp.zeros((blk, blk), F32)
        for p in range(n_pair):
            w_e = ext_t[C_IDX_DIM + 2 * p:C_IDX_DIM + 2 * p + 1, :] * idx_scale
            w_o = ext_t[C_IDX_DIM + 2 * p + 1:C_IDX_DIM + 2 * p + 2, :] * idx_scale
            acc = acc + jnp.maximum(sc_even[:, p * blk:(p + 1) * blk], 0.0) * w_e
            acc = acc + jnp.maximum(sc_odd[:, p * blk:(p + 1) * blk], 0.0) * w_o
        causal = (j * blk + s_io) <= (qb * blk + t_io)
        keys_ref[j] = jnp.where(causal, acc, -jnp.inf)
        mx = jnp.maximum(mx, jnp.max(jnp.where(causal, acc, -jnp.inf), axis=0, keepdims=True))
        mn = jnp.minimum(mn, jnp.min(jnp.where(causal, acc, jnp.inf), axis=0, keepdims=True))
        return mx, mn

    mx, mn = lax.fori_loop(0, qb + 1, idx_body,
                           (jnp.full((1, blk), -jnp.inf, F32), jnp.full((1, blk), jnp.inf, F32)))

    def count(pred):
        def body(j, c):
            return c + jnp.sum(jnp.where(pred(keys_ref[j], j), 1.0, 0.0), axis=0, keepdims=True)
        return lax.fori_loop(0, qb + 1, body, jnp.zeros((1, blk), F32))

    n_causal = (qb * blk + t_io[0:1, :] + 1).astype(F32)
    k_eff = jnp.minimum(n_causal, float(ksel))
    lo0 = mn
    hi0 = mx + (jnp.abs(mx) + 1.0)
    cnt0 = n_causal

    def active_of(lo, hi, cnt):
        mid = lo + 0.5 * (hi - lo)
        collapsed = (mid <= lo) | (mid >= hi)
        return mid, (cnt > k_eff) & jnp.logical_not(collapsed)

    def bis_cond(st):
        return st[3] > 0.0

    def bis_body(st):
        lo, hi, cnt, _ = st
        for _ in range(2):
            mid, active = active_of(lo, hi, cnt)
            c = count(lambda kb, j: kb >= mid)
            ge = c >= k_eff
            lo = jnp.where(active & ge, mid, lo)
            cnt = jnp.where(active & ge, c, cnt)
            hi = jnp.where(active & jnp.logical_not(ge), mid, hi)
        _, active = active_of(lo, hi, cnt)
        return lo, hi, cnt, jnp.sum(jnp.where(active, 1.0, 0.0))

    _, act0 = active_of(lo0, hi0, cnt0)
    thr, _, cnt, _ = lax.while_loop(bis_cond, bis_body,
                                    (lo0, hi0, cnt0, jnp.sum(jnp.where(act0, 1.0, 0.0))))

    def mask_body(j, carry):
        mask_ref[j] = jnp.where(keys_ref[j] >= thr, 0.0, MASK_NEG)
        return carry
    lax.fori_loop(0, qb + 1, mask_body, 0)

    tie = cnt > k_eff

    @pl.when(jnp.sum(jnp.where(tie, 1.0, 0.0)) > 0.0)
    def _():
        need = k_eff - count(lambda kb, j: kb > thr)
        def pos(j):
            return (j * blk + s_io).astype(F32)
        lo_i = jnp.full((1, blk), -1.0, F32)
        hi_i = jnp.zeros((1, blk), F32) + ((qb + 1) * blk - 1).astype(F32)
        n_steps = int(math.ceil(math.log2(seq))) + 1
        def tb(_, st):
            lo_i, hi_i = st
            mid = jnp.floor((lo_i + hi_i) * 0.5)
            c = count(lambda kb, j: (kb == thr) & (pos(j) <= mid))
            ge = c >= need
            return jnp.where(ge, lo_i, mid), jnp.where(ge, mid, hi_i)
        _, cut = lax.fori_loop(0, n_steps, tb, (lo_i, hi_i))
        def fix_body(j, carry):
            kb = keys_ref[j]
            sel = (kb > thr) | ((kb == thr) & (pos(j) <= cut))
            mask_ref[j] = jnp.where(tie, jnp.where(sel, 0.0, MASK_NEG), mask_ref[j])
            return carry
        lax.fori_loop(0, qb + 1, fix_body, 0)

    acc_ref[...] = jnp.zeros_like(acc_ref)
    l_ref[...] = jnp.zeros_like(l_ref)
    m_ref[...] = jnp.full(m_ref.shape, MASK_NEG, F32)
    scale = C_HEAD_DIM ** -0.5

    def attend(j, delta):
        rows = pl.ds(pl.multiple_of(j * blk, blk), blk)
        madd = mask_ref[j]
        for kvh in range(C_KV_HEADS):
            kj = kv_ref[rows, kvh * C_HEAD_DIM:(kvh + 1) * C_HEAD_DIM]
            lg = lax.dot_general(kj, qg_ref[kvh], (((1,), (1,)), ((), ())),
                                 preferred_element_type=F32)
            parts = []
            for g in range(C_GROUP):
                h = kvh * C_GROUP + g
                if delta is None:
                    bias = madd + rb_ref[REL_BUCKETS - 1, h]
                else:
                    bias = madd + bias_ref[h, delta]
                parts.append(lg[:, g * blk:(g + 1) * blk] * scale + bias)
            lg = jnp.concatenate(parts, axis=1)
            m_old = m_ref[kvh]
            m_new = jnp.maximum(m_old, jnp.max(lg, axis=0, keepdims=True))
            alpha = jnp.exp(m_old - m_new)
            p = jnp.exp(lg - m_new)
            l_ref[kvh] = alpha * l_ref[kvh] + jnp.sum(p, axis=0, keepdims=True)
            pv = jnp.dot(vt_ref[kvh, j], p.astype(BF16), preferred_element_type=F32)
            acc_ref[kvh] = acc_ref[kvh] * alpha + pv
            m_ref[kvh] = m_new

    def far_body(j, carry):
        attend(j, None)
        return carry
    lax.fori_loop(0, jnp.maximum(qb - 1, 0), far_body, 0)

    @pl.when(qb >= 1)
    def _():
        attend(qb - 1, 1)

    attend(qb, 0)

    for kvh in range(C_KV_HEADS):
        o_t = acc_ref[kvh] / l_ref[kvh]
        for g in range(C_GROUP):
            h = kvh * C_GROUP + g
            o_ref[:, h * C_HEAD_DIM:(h + 1) * C_HEAD_DIM] = o_t[:, g * blk:(g + 1) * blk].T.astype(o_ref.dtype)


def _dsa(p_main, p_extra, bias_tiles, rel_bias, bsz, seq):
    n = p_main.shape[0]
    nb = seq // C_QBLOCK
    blk = C_QBLOCK
    ksel = min(C_TOPK_MAX, seq // 4)
    n_pair = C_IDX_HEADS // 2
    row = lambda b, i: b * nb + i
    return pl.pallas_call(
        functools.partial(_dsa_kernel, seq=seq, ksel=ksel),
        grid=(bsz, nb),
        in_specs=[
            pl.BlockSpec(memory_space=pltpu.SMEM),
            pl.BlockSpec((blk, C_WIDTH), lambda b, i: (row(b, i), 0)),
            pl.BlockSpec((seq, 2 * C_KVW), lambda b, i: (b, C_WIDTH // (2 * C_KVW))),
            pl.BlockSpec((blk, C_IDX_HEADS * C_IDX_DIM),
                         lambda b, i: (row(b, i), (C_WIDTH + 2 * C_KVW) // (C_IDX_HEADS * C_IDX_DIM))),
            pl.BlockSpec((seq, LANES), lambda b, i: (b, 0)),
            pl.BlockSpec((C_HEADS, 2, blk, blk), lambda b, i: (0, 0, 0, 0)),
        ],
        out_specs=pl.BlockSpec((blk, C_WIDTH), lambda b, i: (row(b, i), 0)),
        out_shape=jax.ShapeDtypeStruct((n, C_WIDTH), BF16),
        scratch_shapes=[
            pltpu.VMEM((nb, blk, blk), F32),
            pltpu.VMEM((nb, blk, blk), F32),
            pltpu.VMEM((C_KV_HEADS, nb, C_HEAD_DIM, blk), BF16),
            pltpu.VMEM((nb, blk, LANES), BF16),
            pltpu.VMEM((nb, blk, LANES), BF16),
            pltpu.VMEM((n_pair * blk, LANES), BF16),
            pltpu.VMEM((C_KV_HEADS, C_GROUP * blk, C_HEAD_DIM), BF16),
            pltpu.VMEM((C_KV_HEADS, C_HEAD_DIM, C_GROUP * blk), F32),
            pltpu.VMEM((C_KV_HEADS, 1, C_GROUP * blk), F32),
            pltpu.VMEM((C_KV_HEADS, 1, C_GROUP * blk), F32),
        ],
        compiler_params=_params("parallel", "arbitrary"),
        name="dsa",
    )(rel_bias, p_main, p_main, p_main, p_extra, bias_tiles)


def _pad_cols(w, width):
    return jnp.pad(w, ((0, 0), (0, width - w.shape[1])))


def _even_mixer(h, g, w_in, v_ln_g, w_s, b_s, gate_w2, gate_b, out_norm_g, w_out, bsz, seq):
    n_main = 2 * A_WIDTH + 2 * B_HEADS * B_DK + 2 * B_WIDTH
    w_main = w_in[:, :n_main].astype(BF16)
    w_extra = _pad_cols(w_in[:, n_main:], LANES).astype(BF16)
    p_main, p_extra = _norm_matmul(h, g, w_main, w_extra, F32)
    a_out = _sgu(p_main, v_ln_g, w_s, b_s.T)
    gate_w2_pad = jnp.pad(gate_w2, ((0, LANES - B_RANK), (0, 0))).astype(BF16)
    b_out = _gla(p_main, p_extra, gate_w2_pad, gate_b, out_norm_g, bsz, seq)
    w_out = w_out.astype(BF16)
    return _proj_residual(h, [a_out, b_out], [w_out[:A_WIDTH], w_out[A_WIDTH:]])


def _odd_mixer(h, g, w_in, w_out, bias_tiles, rel_bias, bsz, seq):
    n_main = C_WIDTH + 2 * C_KVW + C_IDX_HEADS * C_IDX_DIM
    w_main = w_in[:, :n_main].astype(BF16)
    w_extra = _pad_cols(w_in[:, n_main:], LANES).astype(BF16)
    p_main, p_extra = _norm_matmul(h, g, w_main, w_extra, BF16)
    o = _dsa(p_main, p_extra, bias_tiles, rel_bias, bsz, seq)
    return _proj_residual(h, [o], [w_out.astype(BF16)])


def kernel(x, norm_mix_g, norm_ffn_g, final_norm_g, ab_w_in, a_v_ln_g, a_w_s, a_b_s, b_gate_w2,
           b_gate_b, b_out_norm_g, ab_w_out, c_w_in, c_w_out, rel_bias, ffn_w1, ffn_w2):
    bsz, seq, d = x.shape
    depth = norm_mix_g.shape[0]
    h = x.reshape(bsz * seq, d)
    bias_tiles = _bias_tiles(rel_bias)
    for layer in range(depth):
        i = layer // 2
        if layer % 2 == 0:
            h = _even_mixer(h, norm_mix_g[layer], ab_w_in[i], a_v_ln_g[i], a_w_s[i], a_b_s[i],
                            b_gate_w2[i], b_gate_b[i], b_out_norm_g[i], ab_w_out[i], bsz, seq)
        else:
            h = _odd_mixer(h, norm_mix_g[layer], c_w_in[i], c_w_out[i], bias_tiles, rel_bias, bsz, seq)
        h = _ffn(h, norm_ffn_g[layer], ffn_w1[layer].astype(BF16), ffn_w2[layer].astype(BF16),
                 final_norm_g, layer == depth - 1)
    return h.reshape(bsz, seq, d)
```

```python
import functools
import math

import jax
import jax.numpy as jnp
from jax import lax
from jax.experimental import pallas as pl
from jax.experimental.pallas import tpu as pltpu

F32 = jnp.float32
BF16 = jnp.bfloat16
EPS = 1e-6

VMEM_LIMIT_BYTES = 56 * 1024 * 1024
LANES = 128
SUBLANES = 8

A_GROUPS = 8
A_DIM = 128
A_CHUNK = 128
A_WIDTH = A_GROUPS * A_DIM
B_HEADS = 4
B_DK = 128
B_DV = 256
B_RANK = 16
B_TAU = 16.0
B_CHUNK = 64
B_WIDTH = B_HEADS * B_DV
C_HEADS = 16
C_KV_HEADS = 4
C_GROUP = C_HEADS // C_KV_HEADS
C_HEAD_DIM = 128
C_IDX_HEADS = 16
C_IDX_DIM = 64
C_TOPK_MAX = 256
C_QBLOCK = 128
C_WIDTH = C_HEADS * C_HEAD_DIM
C_KVW = C_KV_HEADS * C_HEAD_DIM
C_SEL_ROWS = 4 * C_QBLOCK
C_IDX_ROWS = 2 * C_QBLOCK
REL_BUCKETS = 32
REL_MAX_DIST = 128
MASK_NEG = -1e30


def _params(*sem):
    return pltpu.CompilerParams(dimension_semantics=sem, vmem_limit_bytes=VMEM_LIMIT_BYTES)


def _rms_rows(x, g):
    ms = jnp.mean(x * x, axis=-1, keepdims=True)
    return x * lax.rsqrt(ms + EPS) * g


def _norm_matmul_kernel(x_ref, g_ref, w_ref, we_ref, o_ref, oe_ref, xn_ref):
    @pl.when(pl.program_id(1) == 0)
    def _():
        xn = _rms_rows(x_ref[...], g_ref[...]).astype(BF16)
        xn_ref[...] = xn
        oe_ref[...] = jnp.dot(xn, we_ref[...], preferred_element_type=F32)

    o_ref[...] = jnp.dot(xn_ref[...], w_ref[...], preferred_element_type=F32).astype(o_ref.dtype)


def _norm_matmul(x, g, w_main, w_extra, out_dtype, tm=512, tn=1024):
    n, d = x.shape
    nm = w_main.shape[1]
    return pl.pallas_call(
        _norm_matmul_kernel,
        grid=(n // tm, nm // tn),
        in_specs=[
            pl.BlockSpec((tm, d), lambda i, j: (i, 0)),
            pl.BlockSpec((1, d), lambda i, j: (0, 0)),
            pl.BlockSpec((d, tn), lambda i, j: (0, j)),
            pl.BlockSpec((d, LANES), lambda i, j: (0, 0)),
        ],
        out_specs=[
            pl.BlockSpec((tm, tn), lambda i, j: (i, j)),
            pl.BlockSpec((tm, LANES), lambda i, j: (i, 0)),
        ],
        out_shape=[jax.ShapeDtypeStruct((n, nm), out_dtype), jax.ShapeDtypeStruct((n, LANES), F32)],
        scratch_shapes=[pltpu.VMEM((tm, d), BF16)],
        compiler_params=_params("parallel", "arbitrary"),
        name="norm_matmul",
    )(x, g.reshape(1, d), w_main, w_extra)


def _proj_res_kernel(*refs, n_in):
    h_ref = refs[0]
    o_ref = refs[1 + 2 * n_in]
    acc = h_ref[...]
    for k in range(n_in):
        acc = acc + jnp.dot(refs[1 + k][...], refs[1 + n_in + k][...], preferred_element_type=F32)
    o_ref[...] = acc


def _proj_residual(h, xs, ws, tm=512, tn=1024):
    n, d = h.shape
    n_in = len(xs)
    in_specs = [pl.BlockSpec((tm, tn), lambda i, j: (i, j))]
    in_specs += [pl.BlockSpec((tm, x.shape[1]), lambda i, j: (i, 0)) for x in xs]
    in_specs += [pl.BlockSpec((w.shape[0], tn), lambda i, j: (0, j)) for w in ws]
    return pl.pallas_call(
        functools.partial(_proj_res_kernel, n_in=n_in),
        grid=(n // tm, d // tn),
        in_specs=in_specs,
        out_specs=pl.BlockSpec((tm, tn), lambda i, j: (i, j)),
        out_shape=jax.ShapeDtypeStruct((n, d), F32),
        compiler_params=_params("parallel", "parallel"),
        name="proj_residual",
    )(h, *xs, *ws)


def _ffn_kernel(x_ref, g_ref, w1_ref, w2_ref, fg_ref, o_ref, xn_ref, acc_ref, *, final_norm):
    f = pl.program_id(1)

    @pl.when(f == 0)
    def _():
        xn_ref[...] = _rms_rows(x_ref[...], g_ref[...]).astype(BF16)
        acc_ref[...] = jnp.zeros_like(acc_ref)

    h1 = jnp.dot(xn_ref[...], w1_ref[...], preferred_element_type=F32)
    h1 = jnp.square(jnp.maximum(h1, 0.0)).astype(BF16)
    acc_ref[...] += jnp.dot(h1, w2_ref[...], preferred_element_type=F32)

    @pl.when(f == pl.num_programs(1) - 1)
    def _():
        y = x_ref[...] + acc_ref[...]
        if final_norm:
            y = _rms_rows(y, fg_ref[...])
        o_ref[...] = y


def _ffn(x, g, w1, w2, final_g, final_norm, tm=512, tf=1024):
    n, d = x.shape
    dff = w1.shape[1]
    return pl.pallas_call(
        functools.partial(_ffn_kernel, final_norm=final_norm),
        grid=(n // tm, dff // tf),
        in_specs=[
            pl.BlockSpec((tm, d), lambda i, f: (i, 0)),
            pl.BlockSpec((1, d), lambda i, f: (0, 0)),
            pl.BlockSpec((d, tf), lambda i, f: (0, f)),
            pl.BlockSpec((tf, d), lambda i, f: (f, 0)),
            pl.BlockSpec((1, d), lambda i, f: (0, 0)),
        ],
        out_specs=pl.BlockSpec((tm, d), lambda i, f: (i, 0)),
        out_shape=jax.ShapeDtypeStruct((n, d), F32),
        scratch_shapes=[pltpu.VMEM((tm, d), BF16), pltpu.VMEM((tm, d), F32)],
        compiler_params=_params("parallel", "arbitrary"),
        name="ffn",
    )(x, g.reshape(1, d), w1, w2, final_g.reshape(1, d))


def _gelu(x):
    return 0.5 * x * (1.0 + lax.erf(x * math.sqrt(0.5)))


def _sgu_kernel(u_ref, v_ref, lng_ref, w_ref, bt_ref, o_ref):
    t_i = lax.broadcasted_iota(jnp.int32, (A_CHUNK, A_CHUNK), 0)
    s_i = lax.broadcasted_iota(jnp.int32, (A_CHUNK, A_CHUNK), 1)
    causal = s_i <= t_i
    for g in range(A_GROUPS):
        sl = slice(g * A_DIM, (g + 1) * A_DIM)
        v = _gelu(v_ref[:, sl])
        mu = jnp.mean(v, axis=-1, keepdims=True)
        vc = v - mu
        var = jnp.mean(vc * vc, axis=-1, keepdims=True)
        vn = vc * lax.rsqrt(var + EPS) * lng_ref[g:g + 1, :]
        w = jnp.where(causal, w_ref[g], 0.0).astype(BF16)
        z = jnp.dot(w, vn.astype(BF16), preferred_element_type=F32) + bt_ref[:, g:g + 1]
        o_ref[:, sl] = (_gelu(u_ref[:, sl]) * z).astype(o_ref.dtype)


def _sgu(p_main, ln_g, w_s, b_s_t):
    n = p_main.shape[0]
    return pl.pallas_call(
        _sgu_kernel,
        grid=(n // A_CHUNK,),
        in_specs=[
            pl.BlockSpec((A_CHUNK, A_WIDTH), lambda i: (i, 0)),
            pl.BlockSpec((A_CHUNK, A_WIDTH), lambda i: (i, 1)),
            pl.BlockSpec((A_GROUPS, A_DIM), lambda i: (0, 0)),
            pl.BlockSpec((A_GROUPS, A_CHUNK, A_CHUNK), lambda i: (0, 0, 0)),
            pl.BlockSpec((A_CHUNK, A_GROUPS), lambda i: (0, 0)),
        ],
        out_specs=pl.BlockSpec((A_CHUNK, A_WIDTH), lambda i: (i, 0)),
        out_shape=jax.ShapeDtypeStruct((n, A_WIDTH), BF16),
        compiler_params=_params("parallel"),
        name="sgu",
    )(p_main, p_main, ln_g, w_s, b_s_t)


def _cumsum_rows(x):
    n = x.shape[0]
    row = lax.broadcasted_iota(jnp.int32, x.shape, 0)
    sh = 1
    while sh < n:
        x = x + jnp.where(row >= sh, pltpu.roll(x, sh, axis=0), 0.0)
        sh *= 2
    return x


def _log_sigmoid(x):
    return jnp.minimum(x, 0.0) - jnp.log1p(jnp.exp(-jnp.abs(x)))


def _gla_kernel(q_ref, k_ref, v_ref, r_ref, g_ref, w2_ref, gb_ref, ng_ref, o_ref, s_ref):
    @pl.when(pl.program_id(1) == 0)
    def _():
        s_ref[...] = jnp.zeros_like(s_ref)

    c = B_CHUNK
    gate = jnp.dot(g_ref[...].astype(BF16), w2_ref[...], preferred_element_type=F32) + gb_ref[...]
    log_a = _log_sigmoid(gate) * (1.0 / B_TAU)
    i_i = lax.broadcasted_iota(jnp.int32, (c, c), 0)
    j_i = lax.broadcasted_iota(jnp.int32, (c, c), 1)
    causal = j_i <= i_i
    for h in range(B_HEADS):
        ksl = slice(h * B_DK, (h + 1) * B_DK)
        vsl = slice(h * B_DV, (h + 1) * B_DV)
        cum = _cumsum_rows(log_a[:, ksl])
        last = cum[c - 1:c, :]
        ref = 0.5 * last
        q = q_ref[:, ksl] * (B_DK ** -0.5)
        k = k_ref[:, ksl]
        v = v_ref[:, vsl].astype(BF16)
        qe = (q * jnp.exp(cum - ref)).astype(BF16)
        ke = (k * jnp.exp(ref - cum)).astype(BF16)
        scores = lax.dot_general(qe, ke, (((1,), (1,)), ((), ())), preferred_element_type=F32)
        scores = jnp.where(causal, scores, 0.0).astype(BF16)
        o = jnp.dot(scores, v, preferred_element_type=F32)
        q_inter = (q * jnp.exp(cum)).astype(BF16)
        state = s_ref[h]
        o = o + jnp.dot(q_inter, state.astype(BF16), preferred_element_type=F32)
        cum_t = cum.T
        last_t = cum_t[:, c - 1:c]
        k_state_t = (k.T * jnp.exp(last_t - cum_t)).astype(BF16)
        s_ref[h] = jnp.exp(last_t) * state + jnp.dot(k_state_t, v, preferred_element_type=F32)
        o = _rms_rows(o, ng_ref[...])
        r = r_ref[:, vsl]
        o_ref[:, vsl] = (o * (r * jax.nn.sigmoid(r))).astype(o_ref.dtype)


def _gla(p_main, p_extra, gate_w2_pad, gate_b, norm_g, bsz, seq):
    n = p_main.shape[0]
    nc = seq // B_CHUNK
    c = B_CHUNK
    qk_w = B_HEADS * B_DK
    row = lambda b, i: b * nc + i
    return pl.pallas_call(
        _gla_kernel,
        grid=(bsz, nc),
        in_specs=[
            pl.BlockSpec((c, qk_w), lambda b, i: (row(b, i), (2 * A_WIDTH) // qk_w)),
            pl.BlockSpec((c, qk_w), lambda b, i: (row(b, i), (2 * A_WIDTH) // qk_w + 1)),
            pl.BlockSpec((c, B_WIDTH), lambda b, i: (row(b, i), (2 * A_WIDTH + 2 * qk_w) // B_WIDTH)),
            pl.BlockSpec((c, B_WIDTH), lambda b, i: (row(b, i), (2 * A_WIDTH + 2 * qk_w) // B_WIDTH + 1)),
            pl.BlockSpec((c, LANES), lambda b, i: (row(b, i), 0)),
            pl.BlockSpec((LANES, qk_w), lambda b, i: (0, 0)),
            pl.BlockSpec((1, qk_w), lambda b, i: (0, 0)),
            pl.BlockSpec((1, B_DV), lambda b, i: (0, 0)),
        ],
        out_specs=pl.BlockSpec((c, B_WIDTH), lambda b, i: (row(b, i), 0)),
        out_shape=jax.ShapeDtypeStruct((n, B_WIDTH), BF16),
        scratch_shapes=[pltpu.VMEM((B_HEADS, B_DK, B_DV), F32)],
        compiler_params=_params("parallel", "arbitrary"),
        name="gla",
    )(p_main, p_main, p_main, p_main, p_extra, gate_w2_pad, gate_b.reshape(1, qk_w),
      norm_g.reshape(1, B_DV))


def _t5_bucket(dist):
    max_exact = REL_BUCKETS // 2
    d = jnp.maximum(dist, 1).astype(F32)
    large = max_exact + (jnp.log(d / max_exact) / math.log(REL_MAX_DIST / max_exact)
                         * (REL_BUCKETS - max_exact)).astype(jnp.int32)
    large = jnp.minimum(large, REL_BUCKETS - 1)
    return jnp.where(dist < max_exact, dist, large)


def _bias_kernel(rb_ref, o_ref):
    s_i = lax.broadcasted_iota(jnp.int32, (C_QBLOCK, C_QBLOCK), 0)
    t_i = lax.broadcasted_iota(jnp.int32, (C_QBLOCK, C_QBLOCK), 1)
    for delta in range(2):
        bucket = _t5_bucket(jnp.maximum(delta * C_QBLOCK + t_i - s_i, 0))
        for h in range(C_HEADS):
            acc = jnp.zeros((C_QBLOCK, C_QBLOCK), F32)
            for b in range(REL_BUCKETS):
                acc = jnp.where(bucket == b, rb_ref[b, h], acc)
            o_ref[h, delta] = acc


def _bias_tiles(rel_bias):
    return pl.pallas_call(
        _bias_kernel,
        in_specs=[pl.BlockSpec(memory_space=pltpu.SMEM)],
        out_specs=pl.BlockSpec(memory_space=pltpu.VMEM),
        out_shape=jax.ShapeDtypeStruct((C_HEADS, 2, C_QBLOCK, C_QBLOCK), F32),
        name="bias_tiles",
    )(rel_bias)


def _dsa_kernel(rb_ref, q_ref, kv_ref, iq_ref, ex_ref, bias_ref, o_ref,
                keys_ref, mask_ref, vt_ref, iklo_ref, ikhi_ref, iqa_ref, qg_ref,
                acc_ref, m_ref, l_ref, *, seq, ksel):
    qb = pl.program_id(1)
    blk = C_QBLOCK
    nb = seq // blk
    n_pair = C_IDX_HEADS // 2
    n_sel = lax.shift_right_logical(qb, 2) + 1
    sel_tiles = C_SEL_ROWS // SUBLANES
    lane = lax.broadcasted_iota(jnp.int32, (blk, LANES), 1)
    t_row = lax.broadcasted_iota(jnp.int32, (1, blk), 1)

    @pl.when(qb == 0)
    def _():
        def fill(j, carry):
            rows = pl.ds(pl.multiple_of(j * blk, blk), blk)
            ex = ex_ref[rows, :]
            iklo_ref[rows, :] = jnp.where(lane < C_IDX_DIM, ex, 0.0).astype(BF16)
            ikhi_ref[rows, :] = jnp.where(lane >= C_IDX_DIM, pltpu.roll(ex, C_IDX_DIM, axis=1),
                                          0.0).astype(BF16)
            for kvh in range(C_KV_HEADS):
                vblk = kv_ref[rows, C_KVW + kvh * C_HEAD_DIM:C_KVW + (kvh + 1) * C_HEAD_DIM]
                vt_ref[kvh, j] = vblk.astype(F32).T.astype(BF16)
            return carry
        lax.fori_loop(0, nb, fill, 0)

    q_rows = pl.ds(pl.multiple_of(qb * blk, blk), blk)
    ext_t = ex_ref[q_rows, :].T
    idx_scale = (C_IDX_DIM ** -0.5) * (C_IDX_HEADS ** -0.5)
    w_idx = [ext_t[C_IDX_DIM + h:C_IDX_DIM + h + 1, :] * idx_scale for h in range(C_IDX_HEADS)]
    for p in range(n_pair):
        iqa_ref[p * blk:(p + 1) * blk, :] = iq_ref[:, p * LANES:(p + 1) * LANES]
    for kvh in range(C_KV_HEADS):
        for g in range(C_GROUP):
            h = kvh * C_GROUP + g
            qg_ref[kvh, g * blk:(g + 1) * blk, :] = q_ref[:, h * C_HEAD_DIM:(h + 1) * C_HEAD_DIM]

    s_iu = lax.broadcasted_iota(jnp.int32, (C_IDX_ROWS, blk), 0)
    t_iu = lax.broadcasted_iota(jnp.int32, (C_IDX_ROWS, blk), 1)
    nt = (((1,), (1,)), ((), ()))

    def idx_body(u, carry):
        mx, mn = carry
        rows = pl.ds(pl.multiple_of(u * C_IDX_ROWS, C_IDX_ROWS), C_IDX_ROWS)
        ik_lo = iklo_ref[rows, :]
        ik_hi = ikhi_ref[rows, :]
        acc = jnp.zeros((C_IDX_ROWS, blk), F32)
        for c in range(n_pair // 2):
            w = iqa_ref[c * 2 * blk:(c + 1) * 2 * blk, :]
            sc_even = lax.dot_general(ik_lo, w, nt, preferred_element_type=F32)
            sc_odd = lax.dot_general(ik_hi, w, nt, preferred_element_type=F32)
            for pp in range(2):
                p = 2 * c + pp
                sl = slice(pp * blk, (pp + 1) * blk)
                acc = acc + jnp.maximum(sc_even[:, sl], 0.0) * w_idx[2 * p]
                acc = acc + jnp.maximum(sc_odd[:, sl], 0.0) * w_idx[2 * p + 1]
        causal = (u * C_IDX_ROWS + s_iu) <= (qb * blk + t_iu)
        keys_ref[rows, :] = jnp.where(causal, acc, -jnp.inf)
        mx = jnp.maximum(mx, jnp.max(jnp.where(causal, acc, -jnp.inf), axis=0, keepdims=True))
        mn = jnp.minimum(mn, jnp.min(jnp.where(causal, acc, jnp.inf), axis=0, keepdims=True))
        return mx, mn

    mx, mn = lax.fori_loop(0, n_sel * (C_SEL_ROWS // C_IDX_ROWS), idx_body,
                           (jnp.full((1, blk), -jnp.inf, F32), jnp.full((1, blk), jnp.inf, F32)))

    def sel_rows(i):
        return pl.ds(pl.multiple_of(i * C_SEL_ROWS, C_SEL_ROWS), C_SEL_ROWS)

    def tree_sum(x):
        parts = [x[i] for i in range(x.shape[0])]
        while len(parts) > 1:
            parts = [parts[i] + parts[i + 1] for i in range(0, len(parts), 2)]
        return parts[0]

    def count(pred):
        def body(i, c):
            kb = keys_ref[sel_rows(i), :].reshape(sel_tiles, SUBLANES, blk)
            return c + tree_sum(jnp.where(pred(kb, i), 1.0, 0.0))
        c = lax.fori_loop(0, n_sel, body, jnp.zeros((SUBLANES, blk), F32))
        return jnp.sum(c, axis=0, keepdims=True)

    n_causal = (qb * blk + t_row + 1).astype(F32)
    k_eff = jnp.minimum(n_causal, float(ksel))
    lo0 = mn
    hi0 = mx + (jnp.abs(mx) + 1.0)
    cnt0 = n_causal

    def active_of(lo, hi, cnt):
        mid = lo + 0.5 * (hi - lo)
        collapsed = (mid <= lo) | (mid >= hi)
        return mid, (cnt > k_eff) & jnp.logical_not(collapsed)

    def bis_cond(st):
        return st[3] > 0.0

    def bis_body(st):
        lo, hi, cnt, _ = st
        for _ in range(2):
            mid, active = active_of(lo, hi, cnt)
            c = count(lambda kb, i: kb >= mid)
            ge = c >= k_eff
            lo = jnp.where(active & ge, mid, lo)
            cnt = jnp.where(active & ge, c, cnt)
            hi = jnp.where(active & jnp.logical_not(ge), mid, hi)
        _, active = active_of(lo, hi, cnt)
        return lo, hi, cnt, jnp.sum(jnp.where(active, 1.0, 0.0))

    _, act0 = active_of(lo0, hi0, cnt0)
    thr, _, cnt, _ = lax.while_loop(bis_cond, bis_body,
                                    (lo0, hi0, cnt0, jnp.sum(jnp.where(act0, 1.0, 0.0))))

    def mask_body(i, carry):
        mask_ref[sel_rows(i), :] = jnp.where(keys_ref[sel_rows(i), :] >= thr, 0.0, MASK_NEG)
        return carry
    lax.fori_loop(0, n_sel, mask_body, 0)

    tie = cnt > k_eff

    @pl.when(jnp.sum(jnp.where(tie, 1.0, 0.0)) > 0.0)
    def _():
        tile_i = lax.broadcasted_iota(jnp.int32, (sel_tiles, SUBLANES, blk), 0)
        sub_i = lax.broadcasted_iota(jnp.int32, (sel_tiles, SUBLANES, blk), 1)
        s_i2 = lax.broadcasted_iota(jnp.int32, (C_SEL_ROWS, blk), 0)

        def pos3(i):
            return (i * C_SEL_ROWS + tile_i * SUBLANES + sub_i).astype(F32)

        need = k_eff - count(lambda kb, i: kb > thr)
        lo_i = jnp.full((1, blk), -1.0, F32)
        hi_i = jnp.zeros((1, blk), F32) + (n_sel * C_SEL_ROWS - 1).astype(F32)
        n_steps = int(math.ceil(math.log2(seq))) + 1

        def tb(_, st):
            lo_i, hi_i = st
            mid = jnp.floor((lo_i + hi_i) * 0.5)
            c = count(lambda kb, i: (kb == thr) & (pos3(i) <= mid))
            ge = c >= need
            return jnp.where(ge, lo_i, mid), jnp.where(ge, mid, hi_i)
        _, cut = lax.fori_loop(0, n_steps, tb, (lo_i, hi_i))

        def fix_body(i, carry):
            kb = keys_ref[sel_rows(i), :]
            pos = (i * C_SEL_ROWS + s_i2).astype(F32)
            sel = (kb > thr) | ((kb == thr) & (pos <= cut))
            mask_ref[sel_rows(i), :] = jnp.where(tie, jnp.where(sel, 0.0, MASK_NEG), mask_ref[sel_rows(i), :])
            return carry
        lax.fori_loop(0, n_sel, fix_body, 0)

    acc_ref[...] = jnp.zeros_like(acc_ref)
    l_ref[...] = jnp.zeros_like(l_ref)
    m_ref[...] = jnp.full(m_ref.shape, MASK_NEG, F32)
    scale = C_HEAD_DIM ** -0.5
    inv_scale = 1.0 / scale
    exp_c = scale * math.log2(math.e)

    def attend(j, delta):
        rows = pl.ds(pl.multiple_of(j * blk, blk), blk)
        madd = mask_ref[rows, :]
        results = []
        logits = [lax.dot_general(kv_ref[rows, kvh * C_HEAD_DIM:(kvh + 1) * C_HEAD_DIM], qg_ref[kvh], nt,
                                  preferred_element_type=F32) for kvh in range(C_KV_HEADS)]
        for kvh in range(C_KV_HEADS):
            lg = logits[kvh]
            m_old = m_ref[kvh]
            ys, m_parts, shifts = [], [], []
            for g in range(C_GROUP):
                h = kvh * C_GROUP + g
                sl = slice(g * blk, (g + 1) * blk)
                if delta is None:
                    y = lg[:, sl] + madd
                    off = rb_ref[REL_BUCKETS - 1, h] * inv_scale
                    m_g = jnp.maximum(m_old[:, sl], jnp.max(y, axis=0, keepdims=True) + off)
                    shifts.append(m_g - off)
                else:
                    y = lg[:, sl] + (madd + bias_ref[h, delta] * inv_scale)
                    m_g = jnp.maximum(m_old[:, sl], jnp.max(y, axis=0, keepdims=True))
                    shifts.append(m_g)
                ys.append(y)
                m_parts.append(m_g)
            m_new = jnp.concatenate(m_parts, axis=1)
            alpha = jnp.exp2((m_old - m_new) * exp_c)
            p = jnp.concatenate([jnp.exp2((ys[g] - shifts[g]) * exp_c) for g in range(C_GROUP)], axis=1)
            l_new = alpha * l_ref[kvh] + jnp.sum(p, axis=0, keepdims=True)
            pv = jnp.dot(vt_ref[kvh, j], p.astype(BF16), preferred_element_type=F32)
            results.append((m_new, l_new, acc_ref[kvh] * alpha + pv))
        for kvh, (m_new, l_new, acc_new) in enumerate(results):
            m_ref[kvh] = m_new
            l_ref[kvh] = l_new
            acc_ref[kvh] = acc_new

    def far_body(j, carry):
        attend(j, None)
        return carry
    lax.fori_loop(0, jnp.maximum(qb - 1, 0), far_body, 0)

    @pl.when(qb >= 1)
    def _():
        attend(qb - 1, 1)

    attend(qb, 0)

    for kvh in range(C_KV_HEADS):
        o_t = acc_ref[kvh] / l_ref[kvh]
        for g in range(C_GROUP):
            h = kvh * C_GROUP + g
            o_ref[:, h * C_HEAD_DIM:(h + 1) * C_HEAD_DIM] = o_t[:, g * blk:(g + 1) * blk].T.astype(o_ref.dtype)


def _dsa(p_main, p_extra, bias_tiles, rel_bias, bsz, seq):
    assert seq % C_SEL_ROWS == 0
    n = p_main.shape[0]
    nb = seq // C_QBLOCK
    blk = C_QBLOCK
    ksel = min(C_TOPK_MAX, seq // 4)
    n_pair = C_IDX_HEADS // 2
    row = lambda b, i: b * nb + i
    return pl.pallas_call(
        functools.partial(_dsa_kernel, seq=seq, ksel=ksel),
        grid=(bsz, nb),
        in_specs=[
            pl.BlockSpec(memory_space=pltpu.SMEM),
            pl.BlockSpec((blk, C_WIDTH), lambda b, i: (row(b, i), 0)),
            pl.BlockSpec((seq, 2 * C_KVW), lambda b, i: (b, C_WIDTH // (2 * C_KVW))),
            pl.BlockSpec((blk, C_IDX_HEADS * C_IDX_DIM),
                         lambda b, i: (row(b, i), (C_WIDTH + 2 * C_KVW) // (C_IDX_HEADS * C_IDX_DIM))),
            pl.BlockSpec((seq, LANES), lambda b, i: (b, 0)),
            pl.BlockSpec((C_HEADS, 2, blk, blk), lambda b, i: (0, 0, 0, 0)),
        ],
        out_specs=pl.BlockSpec((blk, C_WIDTH), lambda b, i: (row(b, i), 0)),
        out_shape=jax.ShapeDtypeStruct((n, C_WIDTH), BF16),
        scratch_shapes=[
            pltpu.VMEM((seq, blk), F32),
            pltpu.VMEM((seq, blk), F32),
            pltpu.VMEM((C_KV_HEADS, nb, C_HEAD_DIM, blk), BF16),
            pltpu.VMEM((seq, LANES), BF16),
            pltpu.VMEM((seq, LANES), BF16),
            pltpu.VMEM((n_pair * blk, LANES), BF16),
            pltpu.VMEM((C_KV_HEADS, C_GROUP * blk, C_HEAD_DIM), BF16),
            pltpu.VMEM((C_KV_HEADS, C_HEAD_DIM, C_GROUP * blk), F32),
            pltpu.VMEM((C_KV_HEADS, 1, C_GROUP * blk), F32),
            pltpu.VMEM((C_KV_HEADS, 1, C_GROUP * blk), F32),
        ],
        compiler_params=_params("parallel", "arbitrary"),
        name="dsa",
    )(rel_bias, p_main, p_main, p_main, p_extra, bias_tiles)


def _pad_cols(w, width):
    return jnp.pad(w, ((0, 0), (0, width - w.shape[1])))


def _even_mixer(h, g, w_in, v_ln_g, w_s, b_s, gate_w2, gate_b, out_norm_g, w_out, bsz, seq):
    n_main = 2 * A_WIDTH + 2 * B_HEADS * B_DK + 2 * B_WIDTH
    w_main = w_in[:, :n_main].astype(BF16)
    w_extra = _pad_cols(w_in[:, n_main:], LANES).astype(BF16)
    p_main, p_extra = _norm_matmul(h, g, w_main, w_extra, F32)
    a_out = _sgu(p_main, v_ln_g, w_s, b_s.T)
    gate_w2_pad = jnp.pad(gate_w2, ((0, LANES - B_RANK), (0, 0))).astype(BF16)
    b_out = _gla(p_main, p_extra, gate_w2_pad, gate_b, out_norm_g, bsz, seq)
    w_out = w_out.astype(BF16)
    return _proj_residual(h, [a_out, b_out], [w_out[:A_WIDTH], w_out[A_WIDTH:]])


def _odd_mixer(h, g, w_in, w_out, bias_tiles, rel_bias, bsz, seq):
    n_main = C_WIDTH + 2 * C_KVW + C_IDX_HEADS * C_IDX_DIM
    w_main = w_in[:, :n_main].astype(BF16)
    w_extra = _pad_cols(w_in[:, n_main:], LANES).astype(BF16)
    p_main, p_extra = _norm_matmul(h, g, w_main, w_extra, BF16)
    o = _dsa(p_main, p_extra, bias_tiles, rel_bias, bsz, seq)
    return _proj_residual(h, [o], [w_out.astype(BF16)])


def kernel(x, norm_mix_g, norm_ffn_g, final_norm_g, ab_w_in, a_v_ln_g, a_w_s, a_b_s, b_gate_w2,
           b_gate_b, b_out_norm_g, ab_w_out, c_w_in, c_w_out, rel_bias, ffn_w1, ffn_w2):
    bsz, seq, d = x.shape
    depth = norm_mix_g.shape[0]
    h = x.reshape(bsz * seq, d)
    bias_tiles = _bias_tiles(rel_bias)
    for layer in range(depth):
        i = layer // 2
        if layer % 2 == 0:
            h = _even_mixer(h, norm_mix_g[layer], ab_w_in[i], a_v_ln_g[i], a_w_s[i], a_b_s[i],
                            b_gate_w2[i], b_gate_b[i], b_out_norm_g[i], ab_w_out[i], bsz, seq)
        else:
            h = _odd_mixer(h, norm_mix_g[layer], c_w_in[i], c_w_out[i], bias_tiles, rel_bias, bsz, seq)
        h = _ffn(h, norm_ffn_g[layer], ffn_w1[layer].astype(BF16), ffn_w2[layer].astype(BF16),
                 final_norm_g, layer == depth - 1)
    return h.reshape(bsz, seq, d)
```

```python
import functools
import math

import jax
import jax.numpy as jnp
from jax import lax
from jax.experimental import pallas as pl
from jax.experimental.pallas import tpu as pltpu

F32 = jnp.float32
BF16 = jnp.bfloat16
EPS = 1e-6

VMEM_LIMIT_BYTES = 56 * 1024 * 1024
LANES = 128
SUBLANES = 8

A_GROUPS = 8
A_DIM = 128
A_CHUNK = 128
A_WIDTH = A_GROUPS * A_DIM
B_HEADS = 4
B_DK = 128
B_DV = 256
B_RANK = 16
B_TAU = 16.0
B_CHUNK = 64
B_WIDTH = B_HEADS * B_DV
C_HEADS = 16
C_KV_HEADS = 4
C_GROUP = C_HEADS // C_KV_HEADS
C_HEAD_DIM = 128
C_IDX_HEADS = 16
C_IDX_DIM = 64
C_TOPK_MAX = 256
C_QBLOCK = 128
C_WIDTH = C_HEADS * C_HEAD_DIM
C_KVW = C_KV_HEADS * C_HEAD_DIM
C_SEL_ROWS = 4 * C_QBLOCK
C_IDX_ROWS = 2 * C_QBLOCK
C_VPAD = 16
REL_BUCKETS = 32
REL_MAX_DIST = 128
MASK_NEG = -1e30


def _params(*sem):
    return pltpu.CompilerParams(dimension_semantics=sem, vmem_limit_bytes=VMEM_LIMIT_BYTES)


def _rms_rows(x, g):
    ms = jnp.mean(x * x, axis=-1, keepdims=True)
    return x * lax.rsqrt(ms + EPS) * g


def _norm_matmul_kernel(x_ref, g_ref, w_ref, we_ref, o_ref, oe_ref, xn_ref):
    @pl.when(pl.program_id(1) == 0)
    def _():
        xn = _rms_rows(x_ref[...], g_ref[...]).astype(BF16)
        xn_ref[...] = xn
        oe_ref[...] = jnp.dot(xn, we_ref[...], preferred_element_type=F32)

    o_ref[...] = jnp.dot(xn_ref[...], w_ref[...], preferred_element_type=F32).astype(o_ref.dtype)


def _norm_matmul(x, g, w_main, w_extra, out_dtype, tm=1024, tn=1024):
    n, d = x.shape
    nm = w_main.shape[1]
    return pl.pallas_call(
        _norm_matmul_kernel,
        grid=(n // tm, nm // tn),
        in_specs=[
            pl.BlockSpec((tm, d), lambda i, j: (i, 0)),
            pl.BlockSpec((1, d), lambda i, j: (0, 0)),
            pl.BlockSpec((d, tn), lambda i, j: (0, j)),
            pl.BlockSpec((d, LANES), lambda i, j: (0, 0)),
        ],
        out_specs=[
            pl.BlockSpec((tm, tn), lambda i, j: (i, j)),
            pl.BlockSpec((tm, LANES), lambda i, j: (i, 0)),
        ],
        out_shape=[jax.ShapeDtypeStruct((n, nm), out_dtype), jax.ShapeDtypeStruct((n, LANES), F32)],
        scratch_shapes=[pltpu.VMEM((tm, d), BF16)],
        compiler_params=_params("parallel", "arbitrary"),
        name="norm_matmul",
    )(x, g.reshape(1, d), w_main, w_extra)


def _proj_res_kernel(*refs, n_in):
    h_ref = refs[0]
    o_ref = refs[1 + 2 * n_in]
    acc = h_ref[...]
    for k in range(n_in):
        acc = acc + jnp.dot(refs[1 + k][...], refs[1 + n_in + k][...], preferred_element_type=F32)
    o_ref[...] = acc


def _proj_residual(h, xs, ws, tm=512, tn=2048):
    n, d = h.shape
    n_in = len(xs)
    in_specs = [pl.BlockSpec((tm, tn), lambda i, j: (i, j))]
    in_specs += [pl.BlockSpec((tm, x.shape[1]), lambda i, j: (i, 0)) for x in xs]
    in_specs += [pl.BlockSpec((w.shape[0], tn), lambda i, j: (0, j)) for w in ws]
    return pl.pallas_call(
        functools.partial(_proj_res_kernel, n_in=n_in),
        grid=(n // tm, d // tn),
        in_specs=in_specs,
        out_specs=pl.BlockSpec((tm, tn), lambda i, j: (i, j)),
        out_shape=jax.ShapeDtypeStruct((n, d), F32),
        compiler_params=_params("parallel", "parallel"),
        name="proj_residual",
    )(h, *xs, *ws)


def _ffn_kernel(x_ref, g_ref, w1_ref, w2_ref, fg_ref, o_ref, xn_ref, acc_ref, *, final_norm):
    f = pl.program_id(1)

    @pl.when(f == 0)
    def _():
        xn_ref[...] = _rms_rows(x_ref[...], g_ref[...]).astype(BF16)
        acc_ref[...] = jnp.zeros_like(acc_ref)

    h1 = jnp.dot(xn_ref[...], w1_ref[...], preferred_element_type=F32)
    h1 = jnp.square(jnp.maximum(h1, 0.0)).astype(BF16)
    acc_ref[...] += jnp.dot(h1, w2_ref[...], preferred_element_type=F32)

    @pl.when(f == pl.num_programs(1) - 1)
    def _():
        y = x_ref[...] + acc_ref[...]
        if final_norm:
            y = _rms_rows(y, fg_ref[...])
        o_ref[...] = y


def _ffn(x, g, w1, w2, final_g, final_norm, tm=512, tf=1024):
    n, d = x.shape
    dff = w1.shape[1]
    return pl.pallas_call(
        functools.partial(_ffn_kernel, final_norm=final_norm),
        grid=(n // tm, dff // tf),
        in_specs=[
            pl.BlockSpec((tm, d), lambda i, f: (i, 0)),
            pl.BlockSpec((1, d), lambda i, f: (0, 0)),
            pl.BlockSpec((d, tf), lambda i, f: (0, f)),
            pl.BlockSpec((tf, d), lambda i, f: (f, 0)),
            pl.BlockSpec((1, d), lambda i, f: (0, 0)),
        ],
        out_specs=pl.BlockSpec((tm, d), lambda i, f: (i, 0)),
        out_shape=jax.ShapeDtypeStruct((n, d), F32),
        scratch_shapes=[pltpu.VMEM((tm, d), BF16), pltpu.VMEM((tm, d), F32)],
        compiler_params=_params("parallel", "arbitrary"),
        name="ffn",
    )(x, g.reshape(1, d), w1, w2, final_g.reshape(1, d))


def _gelu(x):
    return 0.5 * x * (1.0 + lax.erf(x * math.sqrt(0.5)))


def _sgu_kernel(u_ref, v_ref, lng_ref, w_ref, bt_ref, o_ref):
    t_i = lax.broadcasted_iota(jnp.int32, (A_CHUNK, A_CHUNK), 0)
    s_i = lax.broadcasted_iota(jnp.int32, (A_CHUNK, A_CHUNK), 1)
    causal = s_i <= t_i
    for g in range(A_GROUPS):
        sl = slice(g * A_DIM, (g + 1) * A_DIM)
        v = _gelu(v_ref[:, sl])
        mu = jnp.mean(v, axis=-1, keepdims=True)
        vc = v - mu
        var = jnp.mean(vc * vc, axis=-1, keepdims=True)
        vn = vc * lax.rsqrt(var + EPS) * lng_ref[g:g + 1, :]
        w = jnp.where(causal, w_ref[g], 0.0).astype(BF16)
        z = jnp.dot(w, vn.astype(BF16), preferred_element_type=F32) + bt_ref[:, g:g + 1]
        o_ref[:, sl] = (_gelu(u_ref[:, sl]) * z).astype(o_ref.dtype)


def _sgu(p_main, ln_g, w_s, b_s_t):
    n = p_main.shape[0]
    return pl.pallas_call(
        _sgu_kernel,
        grid=(n // A_CHUNK,),
        in_specs=[
            pl.BlockSpec((A_CHUNK, A_WIDTH), lambda i: (i, 0)),
            pl.BlockSpec((A_CHUNK, A_WIDTH), lambda i: (i, 1)),
            pl.BlockSpec((A_GROUPS, A_DIM), lambda i: (0, 0)),
            pl.BlockSpec((A_GROUPS, A_CHUNK, A_CHUNK), lambda i: (0, 0, 0)),
            pl.BlockSpec((A_CHUNK, A_GROUPS), lambda i: (0, 0)),
        ],
        out_specs=pl.BlockSpec((A_CHUNK, A_WIDTH), lambda i: (i, 0)),
        out_shape=jax.ShapeDtypeStruct((n, A_WIDTH), BF16),
        compiler_params=_params("parallel"),
        name="sgu",
    )(p_main, p_main, ln_g, w_s, b_s_t)


def _cumsum_rows(x):
    n = x.shape[0]
    row = lax.broadcasted_iota(jnp.int32, x.shape, 0)
    sh = 1
    while sh < n:
        x = x + jnp.where(row >= sh, pltpu.roll(x, sh, axis=0), 0.0)
        sh *= 2
    return x


def _log_sigmoid(x):
    return jnp.minimum(x, 0.0) - jnp.log1p(jnp.exp(-jnp.abs(x)))


def _gla_kernel(q_ref, k_ref, v_ref, r_ref, g_ref, w2_ref, gb_ref, ng_ref, o_ref, s_ref):
    @pl.when(pl.program_id(1) == 0)
    def _():
        s_ref[...] = jnp.zeros_like(s_ref)

    c = B_CHUNK
    gate = jnp.dot(g_ref[...].astype(BF16), w2_ref[...], preferred_element_type=F32) + gb_ref[...]
    log_a = _log_sigmoid(gate) * (1.0 / B_TAU)
    i_i = lax.broadcasted_iota(jnp.int32, (c, c), 0)
    j_i = lax.broadcasted_iota(jnp.int32, (c, c), 1)
    causal = j_i <= i_i
    for h in range(B_HEADS):
        ksl = slice(h * B_DK, (h + 1) * B_DK)
        vsl = slice(h * B_DV, (h + 1) * B_DV)
        cum = _cumsum_rows(log_a[:, ksl])
        last = cum[c - 1:c, :]
        ref = 0.5 * last
        q = q_ref[:, ksl] * (B_DK ** -0.5)
        k = k_ref[:, ksl]
        v = v_ref[:, vsl].astype(BF16)
        qe = (q * jnp.exp(cum - ref)).astype(BF16)
        ke = (k * jnp.exp(ref - cum)).astype(BF16)
        scores = lax.dot_general(qe, ke, (((1,), (1,)), ((), ())), preferred_element_type=F32)
        scores = jnp.where(causal, scores, 0.0).astype(BF16)
        o = jnp.dot(scores, v, preferred_element_type=F32)
        q_inter = (q * jnp.exp(cum)).astype(BF16)
        state = s_ref[h]
        o = o + jnp.dot(q_inter, state.astype(BF16), preferred_element_type=F32)
        cum_t = cum.T
        last_t = cum_t[:, c - 1:c]
        k_state_t = (k.T * jnp.exp(last_t - cum_t)).astype(BF16)
        s_ref[h] = jnp.exp(last_t) * state + jnp.dot(k_state_t, v, preferred_element_type=F32)
        o = _rms_rows(o, ng_ref[...])
        r = r_ref[:, vsl]
        o_ref[:, vsl] = (o * (r * jax.nn.sigmoid(r))).astype(o_ref.dtype)


def _gla(p_main, p_extra, gate_w2_pad, gate_b, norm_g, bsz, seq):
    n = p_main.shape[0]
    nc = seq // B_CHUNK
    c = B_CHUNK
    qk_w = B_HEADS * B_DK
    row = lambda b, i: b * nc + i
    return pl.pallas_call(
        _gla_kernel,
        grid=(bsz, nc),
        in_specs=[
            pl.BlockSpec((c, qk_w), lambda b, i: (row(b, i), (2 * A_WIDTH) // qk_w)),
            pl.BlockSpec((c, qk_w), lambda b, i: (row(b, i), (2 * A_WIDTH) // qk_w + 1)),
            pl.BlockSpec((c, B_WIDTH), lambda b, i: (row(b, i), (2 * A_WIDTH + 2 * qk_w) // B_WIDTH)),
            pl.BlockSpec((c, B_WIDTH), lambda b, i: (row(b, i), (2 * A_WIDTH + 2 * qk_w) // B_WIDTH + 1)),
            pl.BlockSpec((c, LANES), lambda b, i: (row(b, i), 0)),
            pl.BlockSpec((LANES, qk_w), lambda b, i: (0, 0)),
            pl.BlockSpec((1, qk_w), lambda b, i: (0, 0)),
            pl.BlockSpec((1, B_DV), lambda b, i: (0, 0)),
        ],
        out_specs=pl.BlockSpec((c, B_WIDTH), lambda b, i: (row(b, i), 0)),
        out_shape=jax.ShapeDtypeStruct((n, B_WIDTH), BF16),
        scratch_shapes=[pltpu.VMEM((B_HEADS, B_DK, B_DV), F32)],
        compiler_params=_params("parallel", "arbitrary"),
        name="gla",
    )(p_main, p_main, p_main, p_main, p_extra, gate_w2_pad, gate_b.reshape(1, qk_w),
      norm_g.reshape(1, B_DV))


def _t5_bucket(dist):
    max_exact = REL_BUCKETS // 2
    d = jnp.maximum(dist, 1).astype(F32)
    large = max_exact + (jnp.log(d / max_exact) / math.log(REL_MAX_DIST / max_exact)
                         * (REL_BUCKETS - max_exact)).astype(jnp.int32)
    large = jnp.minimum(large, REL_BUCKETS - 1)
    return jnp.where(dist < max_exact, dist, large)


def _bias_kernel(rb_ref, o_ref):
    s_i = lax.broadcasted_iota(jnp.int32, (C_QBLOCK, C_QBLOCK), 0)
    t_i = lax.broadcasted_iota(jnp.int32, (C_QBLOCK, C_QBLOCK), 1)
    for delta in range(2):
        bucket = _t5_bucket(jnp.maximum(delta * C_QBLOCK + t_i - s_i, 0))
        for h in range(C_HEADS):
            acc = jnp.zeros((C_QBLOCK, C_QBLOCK), F32)
            for b in range(REL_BUCKETS):
                acc = jnp.where(bucket == b, rb_ref[b, h], acc)
            o_ref[h, delta] = acc


def _bias_tiles(rel_bias):
    return pl.pallas_call(
        _bias_kernel,
        in_specs=[pl.BlockSpec(memory_space=pltpu.SMEM)],
        out_specs=pl.BlockSpec(memory_space=pltpu.VMEM),
        out_shape=jax.ShapeDtypeStruct((C_HEADS, 2, C_QBLOCK, C_QBLOCK), F32),
        name="bias_tiles",
    )(rel_bias)


def _dsa_kernel(rb_ref, q_ref, kv_ref, iq_ref, ex_ref, bias_ref, o_ref,
                keys_ref, mask_ref, vt_ref, vtw_ref, iklo_ref, ikhi_ref, iqa_ref, qg_ref,
                acc_ref, m_ref, *, seq, ksel):
    qb = pl.program_id(1)
    blk = C_QBLOCK
    nb = seq // blk
    n_pair = C_IDX_HEADS // 2
    n_sel = lax.shift_right_logical(qb, 2) + 1
    sel_tiles = C_SEL_ROWS // SUBLANES
    lane = lax.broadcasted_iota(jnp.int32, (blk, LANES), 1)
    t_row = lax.broadcasted_iota(jnp.int32, (1, blk), 1)
    d_aug = C_HEAD_DIM + C_VPAD

    def ones_row(width):
        r = lax.broadcasted_iota(jnp.int32, (C_VPAD, width), 0)
        return jnp.where(r == 0, 1.0, 0.0).astype(BF16)

    @pl.when(qb == 0)
    def _():
        def fill(j, carry):
            rows = pl.ds(pl.multiple_of(j * blk, blk), blk)
            ex = ex_ref[rows, :]
            iklo_ref[rows, :] = jnp.where(lane < C_IDX_DIM, ex, 0.0).astype(BF16)
            ikhi_ref[rows, :] = jnp.where(lane >= C_IDX_DIM, pltpu.roll(ex, C_IDX_DIM, axis=1),
                                          0.0).astype(BF16)
            for kvh in range(C_KV_HEADS):
                vblk = kv_ref[rows, C_KVW + kvh * C_HEAD_DIM:C_KVW + (kvh + 1) * C_HEAD_DIM]
                vt_ref[kvh, j, 0:C_HEAD_DIM, :] = vblk.astype(F32).T.astype(BF16)
                vt_ref[kvh, j, C_HEAD_DIM:d_aug, :] = ones_row(blk)
            return carry
        lax.fori_loop(0, nb, fill, 0)

        def fill_wide(i, carry):
            rows = pl.ds(pl.multiple_of(i * C_SEL_ROWS, C_SEL_ROWS), C_SEL_ROWS)
            for kvh in range(C_KV_HEADS):
                vblk = kv_ref[rows, C_KVW + kvh * C_HEAD_DIM:C_KVW + (kvh + 1) * C_HEAD_DIM]
                vtw_ref[kvh, i, 0:C_HEAD_DIM, :] = vblk.astype(F32).T.astype(BF16)
                vtw_ref[kvh, i, C_HEAD_DIM:d_aug, :] = ones_row(C_SEL_ROWS)
            return carry
        lax.fori_loop(0, seq // C_SEL_ROWS, fill_wide, 0)

    q_rows = pl.ds(pl.multiple_of(qb * blk, blk), blk)
    ext_t = ex_ref[q_rows, :].T
    idx_scale = (C_IDX_DIM ** -0.5) * (C_IDX_HEADS ** -0.5)
    w_idx = [ext_t[C_IDX_DIM + h:C_IDX_DIM + h + 1, :] * idx_scale for h in range(C_IDX_HEADS)]
    for p in range(n_pair):
        iqa_ref[p * blk:(p + 1) * blk, :] = iq_ref[:, p * LANES:(p + 1) * LANES]
    for kvh in range(C_KV_HEADS):
        for g in range(C_GROUP):
            h = kvh * C_GROUP + g
            qg_ref[kvh, g * blk:(g + 1) * blk, :] = q_ref[:, h * C_HEAD_DIM:(h + 1) * C_HEAD_DIM]

    s_iu = lax.broadcasted_iota(jnp.int32, (C_IDX_ROWS, blk), 0)
    t_iu = lax.broadcasted_iota(jnp.int32, (C_IDX_ROWS, blk), 1)
    nt = (((1,), (1,)), ((), ()))

    def idx_body(u, carry):
        mx, mn = carry
        rows = pl.ds(pl.multiple_of(u * C_IDX_ROWS, C_IDX_ROWS), C_IDX_ROWS)
        ik_lo = iklo_ref[rows, :]
        ik_hi = ikhi_ref[rows, :]
        acc = jnp.zeros((C_IDX_ROWS, blk), F32)
        for c in range(n_pair // 2):
            w = iqa_ref[c * 2 * blk:(c + 1) * 2 * blk, :]
            sc_even = lax.dot_general(ik_lo, w, nt, preferred_element_type=F32)
            sc_odd = lax.dot_general(ik_hi, w, nt, preferred_element_type=F32)
            for pp in range(2):
                p = 2 * c + pp
                sl = slice(pp * blk, (pp + 1) * blk)
                acc = acc + jnp.maximum(sc_even[:, sl], 0.0) * w_idx[2 * p]
                acc = acc + jnp.maximum(sc_odd[:, sl], 0.0) * w_idx[2 * p + 1]
        causal = (u * C_IDX_ROWS + s_iu) <= (qb * blk + t_iu)
        keys_ref[rows, :] = jnp.where(causal, acc, -jnp.inf)
        mx = jnp.maximum(mx, jnp.max(jnp.where(causal, acc, -jnp.inf), axis=0, keepdims=True))
        mn = jnp.minimum(mn, jnp.min(jnp.where(causal, acc, jnp.inf), axis=0, keepdims=True))
        return mx, mn

    mx, mn = lax.fori_loop(0, n_sel * (C_SEL_ROWS // C_IDX_ROWS), idx_body,
                           (jnp.full((1, blk), -jnp.inf, F32), jnp.full((1, blk), jnp.inf, F32)))

    def sel_rows(i):
        return pl.ds(pl.multiple_of(i * C_SEL_ROWS, C_SEL_ROWS), C_SEL_ROWS)

    def tree_sum(x):
        parts = [x[i] for i in range(x.shape[0])]
        while len(parts) > 1:
            parts = [parts[i] + parts[i + 1] for i in range(0, len(parts), 2)]
        return parts[0]

    def count(pred):
        def body(i, c):
            kb = keys_ref[sel_rows(i), :].reshape(sel_tiles, SUBLANES, blk)
            return c + tree_sum(jnp.where(pred(kb, i), 1.0, 0.0))
        c = lax.fori_loop(0, n_sel, body, jnp.zeros((SUBLANES, blk), F32))
        return jnp.sum(c, axis=0, keepdims=True)

    n_causal = (qb * blk + t_row + 1).astype(F32)
    k_eff = jnp.minimum(n_causal, float(ksel))
    lo0 = mn
    hi0 = mx + (jnp.abs(mx) + 1.0)
    cnt0 = n_causal

    def active_of(lo, hi, cnt):
        mid = lo + 0.5 * (hi - lo)
        collapsed = (mid <= lo) | (mid >= hi)
        return mid, (cnt > k_eff) & jnp.logical_not(collapsed)

    def bis_cond(st):
        return st[3] > 0.0

    def bis_body(st):
        lo, hi, cnt, _ = st
        for _ in range(2):
            mid, active = active_of(lo, hi, cnt)
            c = count(lambda kb, i: kb >= mid)
            ge = c >= k_eff
            lo = jnp.where(active & ge, mid, lo)
            cnt = jnp.where(active & ge, c, cnt)
            hi = jnp.where(active & jnp.logical_not(ge), mid, hi)
        _, active = active_of(lo, hi, cnt)
        return lo, hi, cnt, jnp.sum(jnp.where(active, 1.0, 0.0))

    _, act0 = active_of(lo0, hi0, cnt0)
    thr, _, cnt, _ = lax.while_loop(bis_cond, bis_body,
                                    (lo0, hi0, cnt0, jnp.sum(jnp.where(act0, 1.0, 0.0))))

    def mask_body(i, carry):
        mask_ref[sel_rows(i), :] = jnp.where(keys_ref[sel_rows(i), :] >= thr, 0.0, MASK_NEG)
        return carry
    lax.fori_loop(0, n_sel, mask_body, 0)

    tie = cnt > k_eff

    @pl.when(jnp.sum(jnp.where(tie, 1.0, 0.0)) > 0.0)
    def _():
        tile_i = lax.broadcasted_iota(jnp.int32, (sel_tiles, SUBLANES, blk), 0)
        sub_i = lax.broadcasted_iota(jnp.int32, (sel_tiles, SUBLANES, blk), 1)
        s_i2 = lax.broadcasted_iota(jnp.int32, (C_SEL_ROWS, blk), 0)

        def pos3(i):
            return (i * C_SEL_ROWS + tile_i * SUBLANES + sub_i).astype(F32)

        need = k_eff - count(lambda kb, i: kb > thr)
        lo_i = jnp.full((1, blk), -1.0, F32)
        hi_i = jnp.zeros((1, blk), F32) + (n_sel * C_SEL_ROWS - 1).astype(F32)
        n_steps = int(math.ceil(math.log2(seq))) + 1

        def tb(_, st):
            lo_i, hi_i = st
            mid = jnp.floor((lo_i + hi_i) * 0.5)
            c = count(lambda kb, i: (kb == thr) & (pos3(i) <= mid))
            ge = c >= need
            return jnp.where(ge, lo_i, mid), jnp.where(ge, mid, hi_i)
        _, cut = lax.fori_loop(0, n_steps, tb, (lo_i, hi_i))

        def fix_body(i, carry):
            kb = keys_ref[sel_rows(i), :]
            pos = (i * C_SEL_ROWS + s_i2).astype(F32)
            sel = (kb > thr) | ((kb == thr) & (pos <= cut))
            mask_ref[sel_rows(i), :] = jnp.where(tie, jnp.where(sel, 0.0, MASK_NEG), mask_ref[sel_rows(i), :])
            return carry
        lax.fori_loop(0, n_sel, fix_body, 0)

    acc_ref[...] = jnp.zeros_like(acc_ref)
    m_ref[...] = jnp.full(m_ref.shape, MASK_NEG, F32)
    log2e = math.log2(math.e)

    def attend(row0, n_rows, delta, v_tile):
        rows = pl.ds(pl.multiple_of(row0, n_rows), n_rows)
        madd = mask_ref[rows, :]
        results = []
        logits = [lax.dot_general(kv_ref[rows, kvh * C_HEAD_DIM:(kvh + 1) * C_HEAD_DIM], qg_ref[kvh], nt,
                                  preferred_element_type=F32) for kvh in range(C_KV_HEADS)]
        for kvh in range(C_KV_HEADS):
            lg = logits[kvh]
            m_old = m_ref[kvh]
            ys, m_parts, shifts = [], [], []
            for g in range(C_GROUP):
                h = kvh * C_GROUP + g
                sl = slice(g * blk, (g + 1) * blk)
                if delta is None:
                    y = lg[:, sl] + madd
                    off = rb_ref[REL_BUCKETS - 1, h] * log2e
                    m_g = jnp.maximum(m_old[:, sl], jnp.max(y, axis=0, keepdims=True) + off)
                    shifts.append(m_g - off)
                else:
                    y = lg[:, sl] + (madd + bias_ref[h, delta] * log2e)
                    m_g = jnp.maximum(m_old[:, sl], jnp.max(y, axis=0, keepdims=True))
                    shifts.append(m_g)
                ys.append(y)
                m_parts.append(m_g)
            m_new = jnp.concatenate(m_parts, axis=1)
            alpha = jnp.exp2(m_old - m_new)
            p = jnp.concatenate([jnp.exp2(ys[g] - shifts[g]) for g in range(C_GROUP)], axis=1)
            pv = jnp.dot(v_tile(kvh), p.astype(BF16), preferred_element_type=F32)
            results.append((m_new, acc_ref[kvh] * alpha + pv))
        for kvh, (m_new, acc_new) in enumerate(results):
            m_ref[kvh] = m_new
            acc_ref[kvh] = acc_new

    def attend_block(j, delta):
        attend(j * blk, blk, delta, lambda kvh: vt_ref[kvh, j])

    n_far = jnp.maximum(qb - 1, 0)
    n_far_wide = lax.shift_right_logical(n_far, 2)

    def far_wide_body(i, carry):
        attend(i * C_SEL_ROWS, C_SEL_ROWS, None, lambda kvh: vtw_ref[kvh, i])
        return carry
    lax.fori_loop(0, n_far_wide, far_wide_body, 0)

    def far_body(j, carry):
        attend_block(j, None)
        return carry
    lax.fori_loop(n_far_wide * (C_SEL_ROWS // blk), n_far, far_body, 0)

    @pl.when(qb >= 1)
    def _():
        attend_block(qb - 1, 1)

    attend_block(qb, 0)

    for kvh in range(C_KV_HEADS):
        o_t = acc_ref[kvh, 0:C_HEAD_DIM, :] / acc_ref[kvh, C_HEAD_DIM:C_HEAD_DIM + 1, :]
        for g in range(C_GROUP):
            h = kvh * C_GROUP + g
            o_ref[:, h * C_HEAD_DIM:(h + 1) * C_HEAD_DIM] = o_t[:, g * blk:(g + 1) * blk].T.astype(o_ref.dtype)


def _dsa(p_main, p_extra, bias_tiles, rel_bias, bsz, seq):
    assert seq % C_SEL_ROWS == 0
    n = p_main.shape[0]
    nb = seq // C_QBLOCK
    blk = C_QBLOCK
    ksel = min(C_TOPK_MAX, seq // 4)
    n_pair = C_IDX_HEADS // 2
    d_aug = C_HEAD_DIM + C_VPAD
    row = lambda b, i: b * nb + i
    return pl.pallas_call(
        functools.partial(_dsa_kernel, seq=seq, ksel=ksel),
        grid=(bsz, nb),
        in_specs=[
            pl.BlockSpec(memory_space=pltpu.SMEM),
            pl.BlockSpec((blk, C_WIDTH), lambda b, i: (row(b, i), 0)),
            pl.BlockSpec((seq, 2 * C_KVW), lambda b, i: (b, C_WIDTH // (2 * C_KVW))),
            pl.BlockSpec((blk, C_IDX_HEADS * C_IDX_DIM),
                         lambda b, i: (row(b, i), (C_WIDTH + 2 * C_KVW) // (C_IDX_HEADS * C_IDX_DIM))),
            pl.BlockSpec((seq, LANES), lambda b, i: (b, 0)),
            pl.BlockSpec((C_HEADS, 2, blk, blk), lambda b, i: (0, 0, 0, 0)),
        ],
        out_specs=pl.BlockSpec((blk, C_WIDTH), lambda b, i: (row(b, i), 0)),
        out_shape=jax.ShapeDtypeStruct((n, C_WIDTH), BF16),
        scratch_shapes=[
            pltpu.VMEM((seq, blk), F32),
            pltpu.VMEM((seq, blk), F32),
            pltpu.VMEM((C_KV_HEADS, nb, d_aug, blk), BF16),
            pltpu.VMEM((C_KV_HEADS, seq // C_SEL_ROWS, d_aug, C_SEL_ROWS), BF16),
            pltpu.VMEM((seq, LANES), BF16),
            pltpu.VMEM((seq, LANES), BF16),
            pltpu.VMEM((n_pair * blk, LANES), BF16),
            pltpu.VMEM((C_KV_HEADS, C_GROUP * blk, C_HEAD_DIM), BF16),
            pltpu.VMEM((C_KV_HEADS, d_aug, C_GROUP * blk), F32),
            pltpu.VMEM((C_KV_HEADS, 1, C_GROUP * blk), F32),
        ],
        compiler_params=_params("parallel", "arbitrary"),
        name="dsa",
    )(rel_bias, p_main, p_main, p_main, p_extra, bias_tiles)


def _pad_cols(w, width):
    return jnp.pad(w, ((0, 0), (0, width - w.shape[1])))


def _even_mixer(h, g, w_in, v_ln_g, w_s, b_s, gate_w2, gate_b, out_norm_g, w_out, bsz, seq):
    n_main = 2 * A_WIDTH + 2 * B_HEADS * B_DK + 2 * B_WIDTH
    w_main = w_in[:, :n_main].astype(BF16)
    w_extra = _pad_cols(w_in[:, n_main:], LANES).astype(BF16)
    p_main, p_extra = _norm_matmul(h, g, w_main, w_extra, F32)
    a_out = _sgu(p_main, v_ln_g, w_s, b_s.T)
    gate_w2_pad = jnp.pad(gate_w2, ((0, LANES - B_RANK), (0, 0))).astype(BF16)
    b_out = _gla(p_main, p_extra, gate_w2_pad, gate_b, out_norm_g, bsz, seq)
    w_out = w_out.astype(BF16)
    return _proj_residual(h, [a_out, b_out], [w_out[:A_WIDTH], w_out[A_WIDTH:]])


def _odd_mixer(h, g, w_in, w_out, bias_tiles, rel_bias, bsz, seq):
    n_main = C_WIDTH + 2 * C_KVW + C_IDX_HEADS * C_IDX_DIM
    q_scale = (C_HEAD_DIM ** -0.5) * math.log2(math.e)
    col_scale = jnp.where(jnp.arange(n_main) < C_WIDTH, q_scale, 1.0).astype(F32)
    w_main = (w_in[:, :n_main] * col_scale).astype(BF16)
    w_extra = _pad_cols(w_in[:, n_main:], LANES).astype(BF16)
    p_main, p_extra = _norm_matmul(h, g, w_main, w_extra, BF16)
    o = _dsa(p_main, p_extra, bias_tiles, rel_bias, bsz, seq)
    return _proj_residual(h, [o], [w_out.astype(BF16)])


def kernel(x, norm_mix_g, norm_ffn_g, final_norm_g, ab_w_in, a_v_ln_g, a_w_s, a_b_s, b_gate_w2,
           b_gate_b, b_out_norm_g, ab_w_out, c_w_in, c_w_out, rel_bias, ffn_w1, ffn_w2):
    bsz, seq, d = x.shape
    depth = norm_mix_g.shape[0]
    h = x.reshape(bsz * seq, d)
    bias_tiles = _bias_tiles(rel_bias)
    for layer in range(depth):
        i = layer // 2
        if layer % 2 == 0:
            h = _even_mixer(h, norm_mix_g[layer], ab_w_in[i], a_v_ln_g[i], a_w_s[i], a_b_s[i],
                            b_gate_w2[i], b_gate_b[i], b_out_norm_g[i], ab_w_out[i], bsz, seq)
        else:
            h = _odd_mixer(h, norm_mix_g[layer], c_w_in[i], c_w_out[i], bias_tiles, rel_bias, bsz, seq)
        h = _ffn(h, norm_ffn_g[layer], ffn_w1[layer].astype(BF16), ffn_w2[layer].astype(BF16),
                 final_norm_g, layer == depth - 1)
    return h.reshape(bsz, seq, d)
```

```python
import functools
import math

import jax
import jax.numpy as jnp
from jax import lax
from jax.experimental import pallas as pl
from jax.experimental.pallas import tpu as pltpu

F32 = jnp.float32
BF16 = jnp.bfloat16
EPS = 1e-6

VMEM_LIMIT_BYTES = 56 * 1024 * 1024
LANES = 128
SUBLANES = 8

A_GROUPS = 8
A_DIM = 128
A_CHUNK = 128
A_WIDTH = A_GROUPS * A_DIM
B_HEADS = 4
B_DK = 128
B_DV = 256
B_RANK = 16
B_TAU = 16.0
B_CHUNK = 64
B_WIDTH = B_HEADS * B_DV
C_HEADS = 16
C_KV_HEADS = 4
C_GROUP = C_HEADS // C_KV_HEADS
C_HEAD_DIM = 128
C_IDX_HEADS = 16
C_IDX_DIM = 64
C_TOPK_MAX = 256
C_QBLOCK = 128
C_WIDTH = C_HEADS * C_HEAD_DIM
C_KVW = C_KV_HEADS * C_HEAD_DIM
C_SEL_ROWS = 4 * C_QBLOCK
C_IDX_ROWS = 4 * C_QBLOCK
C_ATT_ROWS = 4 * C_QBLOCK
C_VPAD = 16
C_BISECT_WARMUP = 14
REL_BUCKETS = 32
REL_MAX_DIST = 128
MASK_NEG = -1e30


def _params(*sem):
    return pltpu.CompilerParams(dimension_semantics=sem, vmem_limit_bytes=VMEM_LIMIT_BYTES)


def _rms_rows(x, g):
    ms = jnp.mean(x * x, axis=-1, keepdims=True)
    return x * lax.rsqrt(ms + EPS) * g


def _norm_matmul_kernel(x_ref, g_ref, w_ref, we_ref, o_ref, oe_ref, xn_ref):
    @pl.when(pl.program_id(1) == 0)
    def _():
        xn = _rms_rows(x_ref[...], g_ref[...]).astype(BF16)
        xn_ref[...] = xn
        oe_ref[...] = jnp.dot(xn, we_ref[...], preferred_element_type=F32)

    o_ref[...] = jnp.dot(xn_ref[...], w_ref[...], preferred_element_type=F32).astype(o_ref.dtype)


def _norm_matmul(x, g, w_main, w_extra, out_dtype, tm=1024, tn=1024):
    n, d = x.shape
    nm = w_main.shape[1]
    return pl.pallas_call(
        _norm_matmul_kernel,
        grid=(n // tm, nm // tn),
        in_specs=[
            pl.BlockSpec((tm, d), lambda i, j: (i, 0)),
            pl.BlockSpec((1, d), lambda i, j: (0, 0)),
            pl.BlockSpec((d, tn), lambda i, j: (0, j)),
            pl.BlockSpec((d, LANES), lambda i, j: (0, 0)),
        ],
        out_specs=[
            pl.BlockSpec((tm, tn), lambda i, j: (i, j)),
            pl.BlockSpec((tm, LANES), lambda i, j: (i, 0)),
        ],
        out_shape=[jax.ShapeDtypeStruct((n, nm), out_dtype), jax.ShapeDtypeStruct((n, LANES), F32)],
        scratch_shapes=[pltpu.VMEM((tm, d), BF16)],
        compiler_params=_params("parallel", "arbitrary"),
        name="norm_matmul",
    )(x, g.reshape(1, d), w_main, w_extra)


def _proj_res_kernel(*refs, n_in):
    h_ref = refs[0]
    o_ref = refs[1 + 2 * n_in]
    acc = h_ref[...]
    for k in range(n_in):
        acc = acc + jnp.dot(refs[1 + k][...], refs[1 + n_in + k][...], preferred_element_type=F32)
    o_ref[...] = acc


def _proj_residual(h, xs, ws, tm=512, tn=2048):
    n, d = h.shape
    n_in = len(xs)
    in_specs = [pl.BlockSpec((tm, tn), lambda i, j: (i, j))]
    in_specs += [pl.BlockSpec((tm, x.shape[1]), lambda i, j: (i, 0)) for x in xs]
    in_specs += [pl.BlockSpec((w.shape[0], tn), lambda i, j: (0, j)) for w in ws]
    return pl.pallas_call(
        functools.partial(_proj_res_kernel, n_in=n_in),
        grid=(n // tm, d // tn),
        in_specs=in_specs,
        out_specs=pl.BlockSpec((tm, tn), lambda i, j: (i, j)),
        out_shape=jax.ShapeDtypeStruct((n, d), F32),
        compiler_params=_params("parallel", "parallel"),
        name="proj_residual",
    )(h, *xs, *ws)


def _ffn_kernel(x_ref, g_ref, w1_ref, w2_ref, fg_ref, o_ref, xn_ref, acc_ref, *, final_norm):
    f = pl.program_id(1)

    @pl.when(f == 0)
    def _():
        xn_ref[...] = _rms_rows(x_ref[...], g_ref[...]).astype(BF16)
        acc_ref[...] = jnp.zeros_like(acc_ref)

    h1 = jnp.dot(xn_ref[...], w1_ref[...], preferred_element_type=F32)
    h1 = jnp.square(jnp.maximum(h1, 0.0)).astype(BF16)
    acc_ref[...] += jnp.dot(h1, w2_ref[...], preferred_element_type=F32)

    @pl.when(f == pl.num_programs(1) - 1)
    def _():
        y = x_ref[...] + acc_ref[...]
        if final_norm:
            y = _rms_rows(y, fg_ref[...])
        o_ref[...] = y


def _ffn(x, g, w1, w2, final_g, final_norm, tm=512, tf=1024):
    n, d = x.shape
    dff = w1.shape[1]
    return pl.pallas_call(
        functools.partial(_ffn_kernel, final_norm=final_norm),
        grid=(n // tm, dff // tf),
        in_specs=[
            pl.BlockSpec((tm, d), lambda i, f: (i, 0)),
            pl.BlockSpec((1, d), lambda i, f: (0, 0)),
            pl.BlockSpec((d, tf), lambda i, f: (0, f)),
            pl.BlockSpec((tf, d), lambda i, f: (f, 0)),
            pl.BlockSpec((1, d), lambda i, f: (0, 0)),
        ],
        out_specs=pl.BlockSpec((tm, d), lambda i, f: (i, 0)),
        out_shape=jax.ShapeDtypeStruct((n, d), F32),
        scratch_shapes=[pltpu.VMEM((tm, d), BF16), pltpu.VMEM((tm, d), F32)],
        compiler_params=_params("parallel", "arbitrary"),
        name="ffn",
    )(x, g.reshape(1, d), w1, w2, final_g.reshape(1, d))


def _gelu(x):
    return 0.5 * x * (1.0 + lax.erf(x * math.sqrt(0.5)))


def _sgu_kernel(u_ref, v_ref, lng_ref, w_ref, bt_ref, o_ref):
    t_i = lax.broadcasted_iota(jnp.int32, (A_CHUNK, A_CHUNK), 0)
    s_i = lax.broadcasted_iota(jnp.int32, (A_CHUNK, A_CHUNK), 1)
    causal = s_i <= t_i
    for g in range(A_GROUPS):
        sl = slice(g * A_DIM, (g + 1) * A_DIM)
        v = _gelu(v_ref[:, sl])
        mu = jnp.mean(v, axis=-1, keepdims=True)
        vc = v - mu
        var = jnp.mean(vc * vc, axis=-1, keepdims=True)
        vn = vc * lax.rsqrt(var + EPS) * lng_ref[g:g + 1, :]
        w = jnp.where(causal, w_ref[g], 0.0).astype(BF16)
        z = jnp.dot(w, vn.astype(BF16), preferred_element_type=F32) + bt_ref[:, g:g + 1]
        o_ref[:, sl] = (_gelu(u_ref[:, sl]) * z).astype(o_ref.dtype)


def _sgu(p_main, ln_g, w_s, b_s_t):
    n = p_main.shape[0]
    return pl.pallas_call(
        _sgu_kernel,
        grid=(n // A_CHUNK,),
        in_specs=[
            pl.BlockSpec((A_CHUNK, A_WIDTH), lambda i: (i, 0)),
            pl.BlockSpec((A_CHUNK, A_WIDTH), lambda i: (i, 1)),
            pl.BlockSpec((A_GROUPS, A_DIM), lambda i: (0, 0)),
            pl.BlockSpec((A_GROUPS, A_CHUNK, A_CHUNK), lambda i: (0, 0, 0)),
            pl.BlockSpec((A_CHUNK, A_GROUPS), lambda i: (0, 0)),
        ],
        out_specs=pl.BlockSpec((A_CHUNK, A_WIDTH), lambda i: (i, 0)),
        out_shape=jax.ShapeDtypeStruct((n, A_WIDTH), BF16),
        compiler_params=_params("parallel"),
        name="sgu",
    )(p_main, p_main, ln_g, w_s, b_s_t)


def _cumsum_rows(x):
    n = x.shape[0]
    row = lax.broadcasted_iota(jnp.int32, x.shape, 0)
    sh = 1
    while sh < n:
        x = x + jnp.where(row >= sh, pltpu.roll(x, sh, axis=0), 0.0)
        sh *= 2
    return x


def _log_sigmoid(x):
    return jnp.minimum(x, 0.0) - jnp.log1p(jnp.exp(-jnp.abs(x)))


def _gla_kernel(q_ref, k_ref, v_ref, r_ref, g_ref, w2_ref, gb_ref, ng_ref, o_ref, s_ref):
    @pl.when(pl.program_id(1) == 0)
    def _():
        s_ref[...] = jnp.zeros_like(s_ref)

    c = B_CHUNK
    gate = jnp.dot(g_ref[...].astype(BF16), w2_ref[...], preferred_element_type=F32) + gb_ref[...]
    log_a = _log_sigmoid(gate) * (1.0 / B_TAU)
    i_i = lax.broadcasted_iota(jnp.int32, (c, c), 0)
    j_i = lax.broadcasted_iota(jnp.int32, (c, c), 1)
    causal = j_i <= i_i
    for h in range(B_HEADS):
        ksl = slice(h * B_DK, (h + 1) * B_DK)
        vsl = slice(h * B_DV, (h + 1) * B_DV)
        cum = _cumsum_rows(log_a[:, ksl])
        last = cum[c - 1:c, :]
        ref = 0.5 * last
        q = q_ref[:, ksl] * (B_DK ** -0.5)
        k = k_ref[:, ksl]
        v = v_ref[:, vsl].astype(BF16)
        qe = (q * jnp.exp(cum - ref)).astype(BF16)
        ke = (k * jnp.exp(ref - cum)).astype(BF16)
        scores = lax.dot_general(qe, ke, (((1,), (1,)), ((), ())), preferred_element_type=F32)
        scores = jnp.where(causal, scores, 0.0).astype(BF16)
        o = jnp.dot(scores, v, preferred_element_type=F32)
        q_inter = (q * jnp.exp(cum)).astype(BF16)
        state = s_ref[h]
        o = o + jnp.dot(q_inter, state.astype(BF16), preferred_element_type=F32)
        cum_t = cum.T
        last_t = cum_t[:, c - 1:c]
        k_state_t = (k.T * jnp.exp(last_t - cum_t)).astype(BF16)
        s_ref[h] = jnp.exp(last_t) * state + jnp.dot(k_state_t, v, preferred_element_type=F32)
        o = _rms_rows(o, ng_ref[...])
        r = r_ref[:, vsl]
        o_ref[:, vsl] = (o * (r * jax.nn.sigmoid(r))).astype(o_ref.dtype)


def _gla(p_main, p_extra, gate_w2_pad, gate_b, norm_g, bsz, seq):
    n = p_main.shape[0]
    nc = seq // B_CHUNK
    c = B_CHUNK
    qk_w = B_HEADS * B_DK
    row = lambda b, i: b * nc + i
    return pl.pallas_call(
        _gla_kernel,
        grid=(bsz, nc),
        in_specs=[
            pl.BlockSpec((c, qk_w), lambda b, i: (row(b, i), (2 * A_WIDTH) // qk_w)),
            pl.BlockSpec((c, qk_w), lambda b, i: (row(b, i), (2 * A_WIDTH) // qk_w + 1)),
            pl.BlockSpec((c, B_WIDTH), lambda b, i: (row(b, i), (2 * A_WIDTH + 2 * qk_w) // B_WIDTH)),
            pl.BlockSpec((c, B_WIDTH), lambda b, i: (row(b, i), (2 * A_WIDTH + 2 * qk_w) // B_WIDTH + 1)),
            pl.BlockSpec((c, LANES), lambda b, i: (row(b, i), 0)),
            pl.BlockSpec((LANES, qk_w), lambda b, i: (0, 0)),
            pl.BlockSpec((1, qk_w), lambda b, i: (0, 0)),
            pl.BlockSpec((1, B_DV), lambda b, i: (0, 0)),
        ],
        out_specs=pl.BlockSpec((c, B_WIDTH), lambda b, i: (row(b, i), 0)),
        out_shape=jax.ShapeDtypeStruct((n, B_WIDTH), BF16),
        scratch_shapes=[pltpu.VMEM((B_HEADS, B_DK, B_DV), F32)],
        compiler_params=_params("parallel", "arbitrary"),
        name="gla",
    )(p_main, p_main, p_main, p_main, p_extra, gate_w2_pad, gate_b.reshape(1, qk_w),
      norm_g.reshape(1, B_DV))


def _t5_bucket(dist):
    max_exact = REL_BUCKETS // 2
    d = jnp.maximum(dist, 1).astype(F32)
    large = max_exact + (jnp.log(d / max_exact) / math.log(REL_MAX_DIST / max_exact)
                         * (REL_BUCKETS - max_exact)).astype(jnp.int32)
    large = jnp.minimum(large, REL_BUCKETS - 1)
    return jnp.where(dist < max_exact, dist, large)


def _bias_kernel(rb_ref, o_ref):
    s_i = lax.broadcasted_iota(jnp.int32, (C_QBLOCK, C_QBLOCK), 0)
    t_i = lax.broadcasted_iota(jnp.int32, (C_QBLOCK, C_QBLOCK), 1)
    for delta in range(2):
        bucket = _t5_bucket(jnp.maximum(delta * C_QBLOCK + t_i - s_i, 0))
        for h in range(C_HEADS):
            acc = jnp.zeros((C_QBLOCK, C_QBLOCK), F32)
            for b in range(REL_BUCKETS):
                acc = jnp.where(bucket == b, rb_ref[b, h], acc)
            o_ref[h, delta] = acc


def _bias_tiles(rel_bias):
    return pl.pallas_call(
        _bias_kernel,
        in_specs=[pl.BlockSpec(memory_space=pltpu.SMEM)],
        out_specs=pl.BlockSpec(memory_space=pltpu.VMEM),
        out_shape=jax.ShapeDtypeStruct((C_HEADS, 2, C_QBLOCK, C_QBLOCK), F32),
        name="bias_tiles",
    )(rel_bias)


def _dsa_kernel(rb_ref, q_ref, kv_ref, iq_ref, ex_ref, bias_ref, o_ref,
                keys_ref, mask_ref, vt_ref, vtw_ref, iklo_ref, ikhi_ref, iqa_ref, qg_ref,
                acc_ref, m_ref, *, seq, ksel):
    qb = pl.program_id(1)
    blk = C_QBLOCK
    nb = seq // blk
    n_pair = C_IDX_HEADS // 2
    n_sel = lax.shift_right_logical(qb, 2) + 1
    sel_tiles = C_SEL_ROWS // SUBLANES
    lane = lax.broadcasted_iota(jnp.int32, (blk, LANES), 1)
    t_row = lax.broadcasted_iota(jnp.int32, (1, blk), 1)
    d_aug = C_HEAD_DIM + C_VPAD

    def ones_row(width):
        r = lax.broadcasted_iota(jnp.int32, (C_VPAD, width), 0)
        return jnp.where(r == 0, 1.0, 0.0).astype(BF16)

    @pl.when(qb == 0)
    def _():
        def fill(j, carry):
            rows = pl.ds(pl.multiple_of(j * blk, blk), blk)
            ex = ex_ref[rows, :]
            iklo_ref[rows, :] = jnp.where(lane < C_IDX_DIM, ex, 0.0).astype(BF16)
            ikhi_ref[rows, :] = jnp.where(lane >= C_IDX_DIM, pltpu.roll(ex, C_IDX_DIM, axis=1),
                                          0.0).astype(BF16)
            for kvh in range(C_KV_HEADS):
                vblk = kv_ref[rows, C_KVW + kvh * C_HEAD_DIM:C_KVW + (kvh + 1) * C_HEAD_DIM]
                vt_ref[kvh, j, 0:C_HEAD_DIM, :] = vblk.astype(F32).T.astype(BF16)
                vt_ref[kvh, j, C_HEAD_DIM:d_aug, :] = ones_row(blk)
            return carry
        lax.fori_loop(0, nb, fill, 0)

        def fill_wide(i, carry):
            rows = pl.ds(pl.multiple_of(i * C_ATT_ROWS, C_ATT_ROWS), C_ATT_ROWS)
            for kvh in range(C_KV_HEADS):
                vblk = kv_ref[rows, C_KVW + kvh * C_HEAD_DIM:C_KVW + (kvh + 1) * C_HEAD_DIM]
                vtw_ref[kvh, i, 0:C_HEAD_DIM, :] = vblk.astype(F32).T.astype(BF16)
                vtw_ref[kvh, i, C_HEAD_DIM:d_aug, :] = ones_row(C_ATT_ROWS)
            return carry
        lax.fori_loop(0, seq // C_ATT_ROWS, fill_wide, 0)

    q_rows = pl.ds(pl.multiple_of(qb * blk, blk), blk)
    ext_t = ex_ref[q_rows, :].T
    idx_scale = (C_IDX_DIM ** -0.5) * (C_IDX_HEADS ** -0.5)
    w_idx = [ext_t[C_IDX_DIM + h:C_IDX_DIM + h + 1, :] * idx_scale for h in range(C_IDX_HEADS)]
    for p in range(n_pair):
        iqa_ref[p * blk:(p + 1) * blk, :] = iq_ref[:, p * LANES:(p + 1) * LANES]
    for kvh in range(C_KV_HEADS):
        for g in range(C_GROUP):
            h = kvh * C_GROUP + g
            qg_ref[kvh, g * blk:(g + 1) * blk, :] = q_ref[:, h * C_HEAD_DIM:(h + 1) * C_HEAD_DIM]

    s_iu = lax.broadcasted_iota(jnp.int32, (C_IDX_ROWS, blk), 0)
    t_iu = lax.broadcasted_iota(jnp.int32, (C_IDX_ROWS, blk), 1)
    nt = (((1,), (1,)), ((), ()))

    def idx_body(u, carry):
        mx, mn = carry
        rows = pl.ds(pl.multiple_of(u * C_IDX_ROWS, C_IDX_ROWS), C_IDX_ROWS)
        ik_lo = iklo_ref[rows, :]
        ik_hi = ikhi_ref[rows, :]
        acc = jnp.zeros((C_IDX_ROWS, blk), F32)
        for c in range(n_pair // 2):
            w = iqa_ref[c * 2 * blk:(c + 1) * 2 * blk, :]
            sc_even = lax.dot_general(ik_lo, w, nt, preferred_element_type=F32)
            sc_odd = lax.dot_general(ik_hi, w, nt, preferred_element_type=F32)
            for pp in range(2):
                p = 2 * c + pp
                sl = slice(pp * blk, (pp + 1) * blk)
                acc = acc + jnp.maximum(sc_even[:, sl], 0.0) * w_idx[2 * p]
                acc = acc + jnp.maximum(sc_odd[:, sl], 0.0) * w_idx[2 * p + 1]
        causal = (u * C_IDX_ROWS + s_iu) <= (qb * blk + t_iu)
        keys_ref[rows, :] = jnp.where(causal, acc, -jnp.inf)
        mx = jnp.maximum(mx, jnp.max(jnp.where(causal, acc, -jnp.inf), axis=0, keepdims=True))
        mn = jnp.minimum(mn, jnp.min(jnp.where(causal, acc, jnp.inf), axis=0, keepdims=True))
        return mx, mn

    mx, mn = lax.fori_loop(0, n_sel * (C_SEL_ROWS // C_IDX_ROWS), idx_body,
                           (jnp.full((1, blk), -jnp.inf, F32), jnp.full((1, blk), jnp.inf, F32)))

    def sel_rows(i):
        return pl.ds(pl.multiple_of(i * C_SEL_ROWS, C_SEL_ROWS), C_SEL_ROWS)

    def tree_sum(x):
        parts = [x[i] for i in range(x.shape[0])]
        while len(parts) > 1:
            parts = [parts[i] + parts[i + 1] for i in range(0, len(parts), 2)]
        return parts[0]

    def count(pred):
        def body(i, c):
            kb = keys_ref[sel_rows(i), :].reshape(sel_tiles, SUBLANES, blk)
            return c + tree_sum(jnp.where(pred(kb, i), 1.0, 0.0))
        c = lax.fori_loop(0, n_sel, body, jnp.zeros((SUBLANES, blk), F32))
        return jnp.sum(c, axis=0, keepdims=True)

    n_causal = (qb * blk + t_row + 1).astype(F32)
    k_eff = jnp.minimum(n_causal, float(ksel))
    lo0 = mn
    hi0 = mx + (jnp.abs(mx) * 1e-6 + 1e-30)
    cnt0 = n_causal

    def active_of(lo, hi, cnt):
        mid = lo + 0.5 * (hi - lo)
        collapsed = (mid <= lo) | (mid >= hi)
        return mid, (cnt > k_eff) & jnp.logical_not(collapsed)

    def bis_cond(st):
        return st[3] > 0.0

    def bis_step(_, st):
        lo, hi, cnt = st
        mid, active = active_of(lo, hi, cnt)
        c = count(lambda kb, i: kb >= mid)
        ge = c >= k_eff
        return (jnp.where(active & ge, mid, lo), jnp.where(active & jnp.logical_not(ge), mid, hi),
                jnp.where(active & ge, c, cnt))

    def n_active(st):
        _, active = active_of(*st)
        return jnp.sum(jnp.where(active, 1.0, 0.0))

    def bis_body(st):
        st = lax.fori_loop(0, 2, bis_step, st[:3])
        return (*st, n_active(st))

    n_warm = jnp.where(qb * blk >= ksel, C_BISECT_WARMUP, 0)
    st = lax.fori_loop(0, n_warm, bis_step, (lo0, hi0, cnt0))
    thr, _, cnt, _ = lax.while_loop(bis_cond, bis_body, (*st, n_active(st)))

    def mask_body(i, carry):
        mask_ref[sel_rows(i), :] = jnp.where(keys_ref[sel_rows(i), :] >= thr, 0.0, MASK_NEG)
        return carry
    lax.fori_loop(0, n_sel, mask_body, 0)

    tie = cnt > k_eff

    @pl.when(jnp.sum(jnp.where(tie, 1.0, 0.0)) > 0.0)
    def _():
        tile_i = lax.broadcasted_iota(jnp.int32, (sel_tiles, SUBLANES, blk), 0)
        sub_i = lax.broadcasted_iota(jnp.int32, (sel_tiles, SUBLANES, blk), 1)
        s_i2 = lax.broadcasted_iota(jnp.int32, (C_SEL_ROWS, blk), 0)

        def pos3(i):
            return (i * C_SEL_ROWS + tile_i * SUBLANES + sub_i).astype(F32)

        need = k_eff - count(lambda kb, i: kb > thr)
        lo_i = jnp.full((1, blk), -1.0, F32)
        hi_i = jnp.zeros((1, blk), F32) + (n_sel * C_SEL_ROWS - 1).astype(F32)
        n_steps = int(math.ceil(math.log2(seq))) + 1

        def tb(_, st):
            lo_i, hi_i = st
            mid = jnp.floor((lo_i + hi_i) * 0.5)
            c = count(lambda kb, i: (kb == thr) & (pos3(i) <= mid))
            ge = c >= need
            return jnp.where(ge, lo_i, mid), jnp.where(ge, mid, hi_i)
        _, cut = lax.fori_loop(0, n_steps, tb, (lo_i, hi_i))

        def fix_body(i, carry):
            kb = keys_ref[sel_rows(i), :]
            pos = (i * C_SEL_ROWS + s_i2).astype(F32)
            sel = (kb > thr) | ((kb == thr) & (pos <= cut))
            mask_ref[sel_rows(i), :] = jnp.where(tie, jnp.where(sel, 0.0, MASK_NEG), mask_ref[sel_rows(i), :])
            return carry
        lax.fori_loop(0, n_sel, fix_body, 0)

    acc_ref[...] = jnp.zeros_like(acc_ref)
    m_ref[...] = jnp.full(m_ref.shape, MASK_NEG, F32)
    log2e = math.log2(math.e)

    def attend(row0, n_rows, delta, v_tile):
        rows = pl.ds(pl.multiple_of(row0, n_rows), n_rows)
        madd = mask_ref[rows, :]
        results = []
        logits = [lax.dot_general(kv_ref[rows, kvh * C_HEAD_DIM:(kvh + 1) * C_HEAD_DIM], qg_ref[kvh], nt,
                                  preferred_element_type=F32) for kvh in range(C_KV_HEADS)]
        for kvh in range(C_KV_HEADS):
            lg = logits[kvh]
            m_old = m_ref[kvh]
            ys, m_parts, shifts = [], [], []
            for g in range(C_GROUP):
                h = kvh * C_GROUP + g
                sl = slice(g * blk, (g + 1) * blk)
                if delta is None:
                    y = lg[:, sl] + madd
                    off = rb_ref[REL_BUCKETS - 1, h] * log2e
                    m_g = jnp.maximum(m_old[:, sl], jnp.max(y, axis=0, keepdims=True) + off)
                    shifts.append(m_g - off)
                else:
                    y = lg[:, sl] + (madd + bias_ref[h, delta] * log2e)
                    m_g = jnp.maximum(m_old[:, sl], jnp.max(y, axis=0, keepdims=True))
                    shifts.append(m_g)
                ys.append(y)
                m_parts.append(m_g)
            m_new = jnp.concatenate(m_parts, axis=1)
            alpha = jnp.exp2(m_old - m_new)
            p = jnp.concatenate([jnp.exp2(ys[g] - shifts[g]) for g in range(C_GROUP)], axis=1)
            pv = jnp.dot(v_tile(kvh), p.astype(BF16), preferred_element_type=F32)
            results.append((m_new, acc_ref[kvh] * alpha + pv))
        for kvh, (m_new, acc_new) in enumerate(results):
            m_ref[kvh] = m_new
            acc_ref[kvh] = acc_new

    def attend_block(j, delta):
        attend(j * blk, blk, delta, lambda kvh: vt_ref[kvh, j])

    n_far = jnp.maximum(qb - 1, 0)
    n_far_wide = lax.div(n_far, C_ATT_ROWS // blk)

    def far_wide_body(i, carry):
        attend(i * C_ATT_ROWS, C_ATT_ROWS, None, lambda kvh: vtw_ref[kvh, i])
        return carry
    lax.fori_loop(0, n_far_wide, far_wide_body, 0)

    def far_body(j, carry):
        attend_block(j, None)
        return carry
    lax.fori_loop(n_far_wide * (C_ATT_ROWS // blk), n_far, far_body, 0)

    @pl.when(qb >= 1)
    def _():
        attend_block(qb - 1, 1)

    attend_block(qb, 0)

    for kvh in range(C_KV_HEADS):
        o_t = acc_ref[kvh, 0:C_HEAD_DIM, :] / acc_ref[kvh, C_HEAD_DIM:C_HEAD_DIM + 1, :]
        for g in range(C_GROUP):
            h = kvh * C_GROUP + g
            o_ref[:, h * C_HEAD_DIM:(h + 1) * C_HEAD_DIM] = o_t[:, g * blk:(g + 1) * blk].T.astype(o_ref.dtype)


def _dsa(p_main, p_extra, bias_tiles, rel_bias, bsz, seq):
    assert seq % C_SEL_ROWS == 0 and seq % C_ATT_ROWS == 0
    n = p_main.shape[0]
    nb = seq // C_QBLOCK
    blk = C_QBLOCK
    ksel = min(C_TOPK_MAX, seq // 4)
    n_pair = C_IDX_HEADS // 2
    d_aug = C_HEAD_DIM + C_VPAD
    row = lambda b, i: b * nb + i
    return pl.pallas_call(
        functools.partial(_dsa_kernel, seq=seq, ksel=ksel),
        grid=(bsz, nb),
        in_specs=[
            pl.BlockSpec(memory_space=pltpu.SMEM),
            pl.BlockSpec((blk, C_WIDTH), lambda b, i: (row(b, i), 0)),
            pl.BlockSpec((seq, 2 * C_KVW), lambda b, i: (b, C_WIDTH // (2 * C_KVW))),
            pl.BlockSpec((blk, C_IDX_HEADS * C_IDX_DIM),
                         lambda b, i: (row(b, i), (C_WIDTH + 2 * C_KVW) // (C_IDX_HEADS * C_IDX_DIM))),
            pl.BlockSpec((seq, LANES), lambda b, i: (b, 0)),
            pl.BlockSpec((C_HEADS, 2, blk, blk), lambda b, i: (0, 0, 0, 0)),
        ],
        out_specs=pl.BlockSpec((blk, C_WIDTH), lambda b, i: (row(b, i), 0)),
        out_shape=jax.ShapeDtypeStruct((n, C_WIDTH), BF16),
        scratch_shapes=[
            pltpu.VMEM((seq, blk), F32),
            pltpu.VMEM((seq, blk), F32),
            pltpu.VMEM((C_KV_HEADS, nb, d_aug, blk), BF16),
            pltpu.VMEM((C_KV_HEADS, seq // C_ATT_ROWS, d_aug, C_ATT_ROWS), BF16),
            pltpu.VMEM((seq, LANES), BF16),
            pltpu.VMEM((seq, LANES), BF16),
            pltpu.VMEM((n_pair * blk, LANES), BF16),
            pltpu.VMEM((C_KV_HEADS, C_GROUP * blk, C_HEAD_DIM), BF16),
            pltpu.VMEM((C_KV_HEADS, d_aug, C_GROUP * blk), F32),
            pltpu.VMEM((C_KV_HEADS, 1, C_GROUP * blk), F32),
        ],
        compiler_params=_params("parallel", "arbitrary"),
        name="dsa",
    )(rel_bias, p_main, p_main, p_main, p_extra, bias_tiles)


def _pad_cols(w, width):
    return jnp.pad(w, ((0, 0), (0, width - w.shape[1])))


def _even_mixer(h, g, w_in, v_ln_g, w_s, b_s, gate_w2, gate_b, out_norm_g, w_out, bsz, seq):
    n_main = 2 * A_WIDTH + 2 * B_HEADS * B_DK + 2 * B_WIDTH
    w_main = w_in[:, :n_main].astype(BF16)
    w_extra = _pad_cols(w_in[:, n_main:], LANES).astype(BF16)
    p_main, p_extra = _norm_matmul(h, g, w_main, w_extra, F32)
    a_out = _sgu(p_main, v_ln_g, w_s, b_s.T)
    gate_w2_pad = jnp.pad(gate_w2, ((0, LANES - B_RANK), (0, 0))).astype(BF16)
    b_out = _gla(p_main, p_extra, gate_w2_pad, gate_b, out_norm_g, bsz, seq)
    w_out = w_out.astype(BF16)
    return _proj_residual(h, [a_out, b_out], [w_out[:A_WIDTH], w_out[A_WIDTH:]])


def _odd_mixer(h, g, w_in, w_out, bias_tiles, rel_bias, bsz, seq):
    n_main = C_WIDTH + 2 * C_KVW + C_IDX_HEADS * C_IDX_DIM
    q_scale = (C_HEAD_DIM ** -0.5) * math.log2(math.e)
    col_scale = jnp.where(jnp.arange(n_main) < C_WIDTH, q_scale, 1.0).astype(F32)
    w_main = (w_in[:, :n_main] * col_scale).astype(BF16)
    w_extra = _pad_cols(w_in[:, n_main:], LANES).astype(BF16)
    p_main, p_extra = _norm_matmul(h, g, w_main, w_extra, BF16)
    o = _dsa(p_main, p_extra, bias_tiles, rel_bias, bsz, seq)
    return _proj_residual(h, [o], [w_out.astype(BF16)])


def kernel(x, norm_mix_g, norm_ffn_g, final_norm_g, ab_w_in, a_v_ln_g, a_w_s, a_b_s, b_gate_w2,
           b_gate_b, b_out_norm_g, ab_w_out, c_w_in, c_w_out, rel_bias, ffn_w1, ffn_w2):
    bsz, seq, d = x.shape
    depth = norm_mix_g.shape[0]
    h = x.reshape(bsz * seq, d)
    bias_tiles = _bias_tiles(rel_bias)
    for layer in range(depth):
        i = layer // 2
        if layer % 2 == 0:
            h = _even_mixer(h, norm_mix_g[layer], ab_w_in[i], a_v_ln_g[i], a_w_s[i], a_b_s[i],
                            b_gate_w2[i], b_gate_b[i], b_out_norm_g[i], ab_w_out[i], bsz, seq)
        else:
            h = _odd_mixer(h, norm_mix_g[layer], c_w_in[i], c_w_out[i], bias_tiles, rel_bias, bsz, seq)
        h = _ffn(h, norm_ffn_g[layer], ffn_w1[layer].astype(BF16), ffn_w2[layer].astype(BF16),
                 final_norm_g, layer == depth - 1)
    return h.reshape(bsz, seq, d)
```

```python
import functools
import math

import jax
import jax.numpy as jnp
from jax import lax
from jax.experimental import pallas as pl
from jax.experimental.pallas import tpu as pltpu

F32 = jnp.float32
BF16 = jnp.bfloat16
EPS = 1e-6

VMEM_LIMIT_BYTES = 56 * 1024 * 1024
LANES = 128
SUBLANES = 8

A_GROUPS = 8
A_DIM = 128
A_CHUNK = 128
A_WIDTH = A_GROUPS * A_DIM
B_HEADS = 4
B_DK = 128
B_DV = 256
B_RANK = 16
B_TAU = 16.0
B_CHUNK = 64
B_WIDTH = B_HEADS * B_DV
C_HEADS = 16
C_KV_HEADS = 4
C_GROUP = C_HEADS // C_KV_HEADS
C_HEAD_DIM = 128
C_IDX_HEADS = 16
C_IDX_DIM = 64
C_TOPK_MAX = 256
C_QBLOCK = 128
C_WIDTH = C_HEADS * C_HEAD_DIM
C_KVW = C_KV_HEADS * C_HEAD_DIM
C_SEL_ROWS = 4 * C_QBLOCK
C_IDX_ROWS = 4 * C_QBLOCK
C_ATT_ROWS = 4 * C_QBLOCK
C_VPAD = 16
C_BISECT_WARMUP = 14
C_DIRECT_EXP_LIMIT = 60.0
REL_BUCKETS = 32
REL_MAX_DIST = 128
MASK_NEG = -1e30


def _params(*sem):
    return pltpu.CompilerParams(dimension_semantics=sem, vmem_limit_bytes=VMEM_LIMIT_BYTES)


def _rms_rows(x, g):
    ms = jnp.mean(x * x, axis=-1, keepdims=True)
    return x * lax.rsqrt(ms + EPS) * g


def _norm_matmul_kernel(x_ref, g_ref, w_ref, we_ref, o_ref, oe_ref, xn_ref):
    @pl.when(pl.program_id(1) == 0)
    def _():
        xn = _rms_rows(x_ref[...], g_ref[...]).astype(BF16)
        xn_ref[...] = xn
        oe_ref[...] = jnp.dot(xn, we_ref[...], preferred_element_type=F32)

    o_ref[...] = jnp.dot(xn_ref[...], w_ref[...], preferred_element_type=F32).astype(o_ref.dtype)


def _norm_matmul(x, g, w_main, w_extra, out_dtype, tm=1024, tn=1024):
    n, d = x.shape
    nm = w_main.shape[1]
    return pl.pallas_call(
        _norm_matmul_kernel,
        grid=(n // tm, nm // tn),
        in_specs=[
            pl.BlockSpec((tm, d), lambda i, j: (i, 0)),
            pl.BlockSpec((1, d), lambda i, j: (0, 0)),
            pl.BlockSpec((d, tn), lambda i, j: (0, j)),
            pl.BlockSpec((d, LANES), lambda i, j: (0, 0)),
        ],
        out_specs=[
            pl.BlockSpec((tm, tn), lambda i, j: (i, j)),
            pl.BlockSpec((tm, LANES), lambda i, j: (i, 0)),
        ],
        out_shape=[jax.ShapeDtypeStruct((n, nm), out_dtype), jax.ShapeDtypeStruct((n, LANES), F32)],
        scratch_shapes=[pltpu.VMEM((tm, d), BF16)],
        compiler_params=_params("parallel", "arbitrary"),
        name="norm_matmul",
    )(x, g.reshape(1, d), w_main, w_extra)


def _proj_res_kernel(*refs, n_in):
    h_ref = refs[0]
    o_ref = refs[1 + 2 * n_in]
    acc = h_ref[...]
    for k in range(n_in):
        acc = acc + jnp.dot(refs[1 + k][...], refs[1 + n_in + k][...], preferred_element_type=F32)
    o_ref[...] = acc


def _proj_residual(h, xs, ws, tm=512, tn=2048):
    n, d = h.shape
    n_in = len(xs)
    in_specs = [pl.BlockSpec((tm, tn), lambda i, j: (i, j))]
    in_specs += [pl.BlockSpec((tm, x.shape[1]), lambda i, j: (i, 0)) for x in xs]
    in_specs += [pl.BlockSpec((w.shape[0], tn), lambda i, j: (0, j)) for w in ws]
    return pl.pallas_call(
        functools.partial(_proj_res_kernel, n_in=n_in),
        grid=(n // tm, d // tn),
        in_specs=in_specs,
        out_specs=pl.BlockSpec((tm, tn), lambda i, j: (i, j)),
        out_shape=jax.ShapeDtypeStruct((n, d), F32),
        compiler_params=_params("parallel", "parallel"),
        name="proj_residual",
    )(h, *xs, *ws)


def _ffn_kernel(x_ref, g_ref, w1_ref, w2_ref, fg_ref, o_ref, xn_ref, acc_ref, *, final_norm):
    f = pl.program_id(1)

    @pl.when(f == 0)
    def _():
        xn_ref[...] = _rms_rows(x_ref[...], g_ref[...]).astype(BF16)
        acc_ref[...] = jnp.zeros_like(acc_ref)

    h1 = jnp.dot(xn_ref[...], w1_ref[...], preferred_element_type=F32)
    h1 = jnp.square(jnp.maximum(h1, 0.0)).astype(BF16)
    acc_ref[...] += jnp.dot(h1, w2_ref[...], preferred_element_type=F32)

    @pl.when(f == pl.num_programs(1) - 1)
    def _():
        y = x_ref[...] + acc_ref[...]
        if final_norm:
            y = _rms_rows(y, fg_ref[...])
        o_ref[...] = y


def _ffn(x, g, w1, w2, final_g, final_norm, tm=512, tf=1024):
    n, d = x.shape
    dff = w1.shape[1]
    return pl.pallas_call(
        functools.partial(_ffn_kernel, final_norm=final_norm),
        grid=(n // tm, dff // tf),
        in_specs=[
            pl.BlockSpec((tm, d), lambda i, f: (i, 0)),
            pl.BlockSpec((1, d), lambda i, f: (0, 0)),
            pl.BlockSpec((d, tf), lambda i, f: (0, f)),
            pl.BlockSpec((tf, d), lambda i, f: (f, 0)),
            pl.BlockSpec((1, d), lambda i, f: (0, 0)),
        ],
        out_specs=pl.BlockSpec((tm, d), lambda i, f: (i, 0)),
        out_shape=jax.ShapeDtypeStruct((n, d), F32),
        scratch_shapes=[pltpu.VMEM((tm, d), BF16), pltpu.VMEM((tm, d), F32)],
        compiler_params=_params("parallel", "arbitrary"),
        name="ffn",
    )(x, g.reshape(1, d), w1, w2, final_g.reshape(1, d))


def _gelu(x):
    return 0.5 * x * (1.0 + lax.erf(x * math.sqrt(0.5)))


def _sgu_kernel(u_ref, v_ref, lng_ref, w_ref, bt_ref, o_ref):
    t_i = lax.broadcasted_iota(jnp.int32, (A_CHUNK, A_CHUNK), 0)
    s_i = lax.broadcasted_iota(jnp.int32, (A_CHUNK, A_CHUNK), 1)
    causal = s_i <= t_i
    for g in range(A_GROUPS):
        sl = slice(g * A_DIM, (g + 1) * A_DIM)
        v = _gelu(v_ref[:, sl])
        mu = jnp.mean(v, axis=-1, keepdims=True)
        vc = v - mu
        var = jnp.mean(vc * vc, axis=-1, keepdims=True)
        vn = vc * lax.rsqrt(var + EPS) * lng_ref[g:g + 1, :]
        w = jnp.where(causal, w_ref[g], 0.0).astype(BF16)
        z = jnp.dot(w, vn.astype(BF16), preferred_element_type=F32) + bt_ref[:, g:g + 1]
        o_ref[:, sl] = (_gelu(u_ref[:, sl]) * z).astype(o_ref.dtype)


def _sgu(p_main, ln_g, w_s, b_s_t):
    n = p_main.shape[0]
    return pl.pallas_call(
        _sgu_kernel,
        grid=(n // A_CHUNK,),
        in_specs=[
            pl.BlockSpec((A_CHUNK, A_WIDTH), lambda i: (i, 0)),
            pl.BlockSpec((A_CHUNK, A_WIDTH), lambda i: (i, 1)),
            pl.BlockSpec((A_GROUPS, A_DIM), lambda i: (0, 0)),
            pl.BlockSpec((A_GROUPS, A_CHUNK, A_CHUNK), lambda i: (0, 0, 0)),
            pl.BlockSpec((A_CHUNK, A_GROUPS), lambda i: (0, 0)),
        ],
        out_specs=pl.BlockSpec((A_CHUNK, A_WIDTH), lambda i: (i, 0)),
        out_shape=jax.ShapeDtypeStruct((n, A_WIDTH), BF16),
        compiler_params=_params("parallel"),
        name="sgu",
    )(p_main, p_main, ln_g, w_s, b_s_t)


def _cumsum_rows(x):
    n = x.shape[0]
    row = lax.broadcasted_iota(jnp.int32, x.shape, 0)
    sh = 1
    while sh < n:
        x = x + jnp.where(row >= sh, pltpu.roll(x, sh, axis=0), 0.0)
        sh *= 2
    return x


def _log_sigmoid(x):
    return jnp.minimum(x, 0.0) - jnp.log1p(jnp.exp(-jnp.abs(x)))


def _gla_kernel(q_ref, k_ref, v_ref, r_ref, g_ref, w2_ref, gb_ref, ng_ref, o_ref, s_ref):
    @pl.when(pl.program_id(1) == 0)
    def _():
        s_ref[...] = jnp.zeros_like(s_ref)

    c = B_CHUNK
    gate = jnp.dot(g_ref[...].astype(BF16), w2_ref[...], preferred_element_type=F32) + gb_ref[...]
    log_a = _log_sigmoid(gate) * (1.0 / B_TAU)
    i_i = lax.broadcasted_iota(jnp.int32, (c, c), 0)
    j_i = lax.broadcasted_iota(jnp.int32, (c, c), 1)
    causal = j_i <= i_i
    for h in range(B_HEADS):
        ksl = slice(h * B_DK, (h + 1) * B_DK)
        vsl = slice(h * B_DV, (h + 1) * B_DV)
        cum = _cumsum_rows(log_a[:, ksl])
        last = cum[c - 1:c, :]
        ref = 0.5 * last
        q = q_ref[:, ksl] * (B_DK ** -0.5)
        k = k_ref[:, ksl]
        v = v_ref[:, vsl].astype(BF16)
        qe = (q * jnp.exp(cum - ref)).astype(BF16)
        ke = (k * jnp.exp(ref - cum)).astype(BF16)
        scores = lax.dot_general(qe, ke, (((1,), (1,)), ((), ())), preferred_element_type=F32)
        scores = jnp.where(causal, scores, 0.0).astype(BF16)
        o = jnp.dot(scores, v, preferred_element_type=F32)
        q_inter = (q * jnp.exp(cum)).astype(BF16)
        state = s_ref[h]
        o = o + jnp.dot(q_inter, state.astype(BF16), preferred_element_type=F32)
        cum_t = cum.T
        last_t = cum_t[:, c - 1:c]
        k_state_t = (k.T * jnp.exp(last_t - cum_t)).astype(BF16)
        s_ref[h] = jnp.exp(last_t) * state + jnp.dot(k_state_t, v, preferred_element_type=F32)
        o = _rms_rows(o, ng_ref[...])
        r = r_ref[:, vsl]
        o_ref[:, vsl] = (o * (r * jax.nn.sigmoid(r))).astype(o_ref.dtype)


def _gla(p_main, p_extra, gate_w2_pad, gate_b, norm_g, bsz, seq):
    n = p_main.shape[0]
    nc = seq // B_CHUNK
    c = B_CHUNK
    qk_w = B_HEADS * B_DK
    row = lambda b, i: b * nc + i
    return pl.pallas_call(
        _gla_kernel,
        grid=(bsz, nc),
        in_specs=[
            pl.BlockSpec((c, qk_w), lambda b, i: (row(b, i), (2 * A_WIDTH) // qk_w)),
            pl.BlockSpec((c, qk_w), lambda b, i: (row(b, i), (2 * A_WIDTH) // qk_w + 1)),
            pl.BlockSpec((c, B_WIDTH), lambda b, i: (row(b, i), (2 * A_WIDTH + 2 * qk_w) // B_WIDTH)),
            pl.BlockSpec((c, B_WIDTH), lambda b, i: (row(b, i), (2 * A_WIDTH + 2 * qk_w) // B_WIDTH + 1)),
            pl.BlockSpec((c, LANES), lambda b, i: (row(b, i), 0)),
            pl.BlockSpec((LANES, qk_w), lambda b, i: (0, 0)),
            pl.BlockSpec((1, qk_w), lambda b, i: (0, 0)),
            pl.BlockSpec((1, B_DV), lambda b, i: (0, 0)),
        ],
        out_specs=pl.BlockSpec((c, B_WIDTH), lambda b, i: (row(b, i), 0)),
        out_shape=jax.ShapeDtypeStruct((n, B_WIDTH), BF16),
        scratch_shapes=[pltpu.VMEM((B_HEADS, B_DK, B_DV), F32)],
        compiler_params=_params("parallel", "arbitrary"),
        name="gla",
    )(p_main, p_main, p_main, p_main, p_extra, gate_w2_pad, gate_b.reshape(1, qk_w),
      norm_g.reshape(1, B_DV))


def _t5_bucket(dist):
    max_exact = REL_BUCKETS // 2
    d = jnp.maximum(dist, 1).astype(F32)
    large = max_exact + (jnp.log(d / max_exact) / math.log(REL_MAX_DIST / max_exact)
                         * (REL_BUCKETS - max_exact)).astype(jnp.int32)
    large = jnp.minimum(large, REL_BUCKETS - 1)
    return jnp.where(dist < max_exact, dist, large)


def _bias_kernel(rb_ref, o_ref):
    s_i = lax.broadcasted_iota(jnp.int32, (C_QBLOCK, C_QBLOCK), 0)
    t_i = lax.broadcasted_iota(jnp.int32, (C_QBLOCK, C_QBLOCK), 1)
    for delta in range(2):
        bucket = _t5_bucket(jnp.maximum(delta * C_QBLOCK + t_i - s_i, 0))
        for h in range(C_HEADS):
            acc = jnp.zeros((C_QBLOCK, C_QBLOCK), F32)
            for b in range(REL_BUCKETS):
                acc = jnp.where(bucket == b, rb_ref[b, h], acc)
            o_ref[h, delta] = acc


def _bias_tiles(rel_bias):
    return pl.pallas_call(
        _bias_kernel,
        in_specs=[pl.BlockSpec(memory_space=pltpu.SMEM)],
        out_specs=pl.BlockSpec(memory_space=pltpu.VMEM),
        out_shape=jax.ShapeDtypeStruct((C_HEADS, 2, C_QBLOCK, C_QBLOCK), F32),
        name="bias_tiles",
    )(rel_bias)


def _dsa_kernel(rb_ref, q_ref, kv_ref, iq_ref, ex_ref, bias_ref, o_ref,
                keys_ref, mask_ref, vt_ref, vtw_ref, iklo_ref, ikhi_ref, iqa_ref, qg_ref,
                acc_ref, m_ref, knorm_ref, bmax_ref, *, seq, ksel):
    qb = pl.program_id(1)
    blk = C_QBLOCK
    nb = seq // blk
    n_pair = C_IDX_HEADS // 2
    n_sel = lax.shift_right_logical(qb, 2) + 1
    sel_tiles = C_SEL_ROWS // SUBLANES
    lane = lax.broadcasted_iota(jnp.int32, (blk, LANES), 1)
    t_row = lax.broadcasted_iota(jnp.int32, (1, blk), 1)
    d_aug = C_HEAD_DIM + C_VPAD

    def ones_row(width):
        r = lax.broadcasted_iota(jnp.int32, (C_VPAD, width), 0)
        return jnp.where(r == 0, 1.0, 0.0).astype(BF16)

    @pl.when(qb == 0)
    def _():
        def fill(j, k_sq):
            rows = pl.ds(pl.multiple_of(j * blk, blk), blk)
            ex = ex_ref[rows, :]
            iklo_ref[rows, :] = jnp.where(lane < C_IDX_DIM, ex, 0.0).astype(BF16)
            ikhi_ref[rows, :] = jnp.where(lane >= C_IDX_DIM, pltpu.roll(ex, C_IDX_DIM, axis=1),
                                          0.0).astype(BF16)
            for kvh in range(C_KV_HEADS):
                vblk = kv_ref[rows, C_KVW + kvh * C_HEAD_DIM:C_KVW + (kvh + 1) * C_HEAD_DIM]
                vt_ref[kvh, j, 0:C_HEAD_DIM, :] = vblk.astype(F32).T.astype(BF16)
                vt_ref[kvh, j, C_HEAD_DIM:d_aug, :] = ones_row(blk)
            k_sq_new = []
            for kvh in range(C_KV_HEADS):
                kblk = kv_ref[rows, kvh * C_HEAD_DIM:(kvh + 1) * C_HEAD_DIM].astype(F32)
                k_sq_new.append(jnp.maximum(k_sq[kvh], jnp.sum(kblk * kblk, axis=1, keepdims=True)))
            return tuple(k_sq_new)
        k_sq = lax.fori_loop(0, nb, fill, tuple(jnp.zeros((blk, 1), F32) for _ in range(C_KV_HEADS)))
        for kvh in range(C_KV_HEADS):
            knorm_ref[kvh] = jnp.max(jnp.sqrt(k_sq[kvh]))
        for h in range(C_HEADS):
            b_abs = jnp.abs(rb_ref[0, h])
            for b in range(1, REL_BUCKETS):
                b_abs = jnp.maximum(b_abs, jnp.abs(rb_ref[b, h]))
            bmax_ref[h] = b_abs * math.log2(math.e)

        def fill_wide(i, carry):
            rows = pl.ds(pl.multiple_of(i * C_ATT_ROWS, C_ATT_ROWS), C_ATT_ROWS)
            for kvh in range(C_KV_HEADS):
                vblk = kv_ref[rows, C_KVW + kvh * C_HEAD_DIM:C_KVW + (kvh + 1) * C_HEAD_DIM]
                vtw_ref[kvh, i, 0:C_HEAD_DIM, :] = vblk.astype(F32).T.astype(BF16)
                vtw_ref[kvh, i, C_HEAD_DIM:d_aug, :] = ones_row(C_ATT_ROWS)
            return carry
        lax.fori_loop(0, seq // C_ATT_ROWS, fill_wide, 0)

    q_rows = pl.ds(pl.multiple_of(qb * blk, blk), blk)
    ext_t = ex_ref[q_rows, :].T
    idx_scale = (C_IDX_DIM ** -0.5) * (C_IDX_HEADS ** -0.5)
    w_idx = [ext_t[C_IDX_DIM + h:C_IDX_DIM + h + 1, :] * idx_scale for h in range(C_IDX_HEADS)]
    for p in range(n_pair):
        iqa_ref[p * blk:(p + 1) * blk, :] = iq_ref[:, p * LANES:(p + 1) * LANES]
    for kvh in range(C_KV_HEADS):
        for g in range(C_GROUP):
            h = kvh * C_GROUP + g
            qg_ref[kvh, g * blk:(g + 1) * blk, :] = q_ref[:, h * C_HEAD_DIM:(h + 1) * C_HEAD_DIM]

    s_iu = lax.broadcasted_iota(jnp.int32, (C_IDX_ROWS, blk), 0)
    t_iu = lax.broadcasted_iota(jnp.int32, (C_IDX_ROWS, blk), 1)
    nt = (((1,), (1,)), ((), ()))

    def idx_body(u, carry):
        mx, mn = carry
        rows = pl.ds(pl.multiple_of(u * C_IDX_ROWS, C_IDX_ROWS), C_IDX_ROWS)
        ik_lo = iklo_ref[rows, :]
        ik_hi = ikhi_ref[rows, :]
        acc = jnp.zeros((C_IDX_ROWS, blk), F32)
        for c in range(n_pair // 2):
            w = iqa_ref[c * 2 * blk:(c + 1) * 2 * blk, :]
            sc_even = lax.dot_general(ik_lo, w, nt, preferred_element_type=F32)
            sc_odd = lax.dot_general(ik_hi, w, nt, preferred_element_type=F32)
            for pp in range(2):
                p = 2 * c + pp
                sl = slice(pp * blk, (pp + 1) * blk)
                acc = acc + jnp.maximum(sc_even[:, sl], 0.0) * w_idx[2 * p]
                acc = acc + jnp.maximum(sc_odd[:, sl], 0.0) * w_idx[2 * p + 1]
        causal = (u * C_IDX_ROWS + s_iu) <= (qb * blk + t_iu)
        keys_ref[rows, :] = jnp.where(causal, acc, -jnp.inf)
        mx = jnp.maximum(mx, jnp.max(jnp.where(causal, acc, -jnp.inf), axis=0, keepdims=True))
        mn = jnp.minimum(mn, jnp.min(jnp.where(causal, acc, jnp.inf), axis=0, keepdims=True))
        return mx, mn

    mx, mn = lax.fori_loop(0, n_sel * (C_SEL_ROWS // C_IDX_ROWS), idx_body,
                           (jnp.full((1, blk), -jnp.inf, F32), jnp.full((1, blk), jnp.inf, F32)))

    def sel_rows(i):
        return pl.ds(pl.multiple_of(i * C_SEL_ROWS, C_SEL_ROWS), C_SEL_ROWS)

    def tree_sum(x):
        parts = [x[i] for i in range(x.shape[0])]
        while len(parts) > 1:
            parts = [parts[i] + parts[i + 1] for i in range(0, len(parts), 2)]
        return parts[0]

    def count(pred):
        def body(i, c):
            kb = keys_ref[sel_rows(i), :].reshape(sel_tiles, SUBLANES, blk)
            return c + tree_sum(jnp.where(pred(kb, i), 1.0, 0.0))
        c = lax.fori_loop(0, n_sel, body, jnp.zeros((SUBLANES, blk), F32))
        return jnp.sum(c, axis=0, keepdims=True)

    n_causal = (qb * blk + t_row + 1).astype(F32)
    k_eff = jnp.minimum(n_causal, float(ksel))
    lo0 = mn
    hi0 = mx + (jnp.abs(mx) * 1e-6 + 1e-30)
    cnt0 = n_causal

    def active_of(lo, hi, cnt):
        mid = lo + 0.5 * (hi - lo)
        collapsed = (mid <= lo) | (mid >= hi)
        return mid, (cnt > k_eff) & jnp.logical_not(collapsed)

    def bis_cond(st):
        return st[3] > 0.0

    def bis_step(_, st):
        lo, hi, cnt = st
        mid, active = active_of(lo, hi, cnt)
        c = count(lambda kb, i: kb >= mid)
        ge = c >= k_eff
        return (jnp.where(active & ge, mid, lo), jnp.where(active & jnp.logical_not(ge), mid, hi),
                jnp.where(active & ge, c, cnt))

    def n_active(st):
        _, active = active_of(*st)
        return jnp.sum(jnp.where(active, 1.0, 0.0))

    def bis_body(st):
        st = lax.fori_loop(0, 2, bis_step, st[:3])
        return (*st, n_active(st))

    n_warm = jnp.where(qb * blk >= ksel, C_BISECT_WARMUP, 0)
    st = lax.fori_loop(0, n_warm, bis_step, (lo0, hi0, cnt0))
    thr, _, cnt, _ = lax.while_loop(bis_cond, bis_body, (*st, n_active(st)))

    def mask_body(i, carry):
        mask_ref[sel_rows(i), :] = jnp.where(keys_ref[sel_rows(i), :] >= thr, 0.0, MASK_NEG)
        return carry
    lax.fori_loop(0, n_sel, mask_body, 0)

    tie = cnt > k_eff

    @pl.when(jnp.sum(jnp.where(tie, 1.0, 0.0)) > 0.0)
    def _():
        tile_i = lax.broadcasted_iota(jnp.int32, (sel_tiles, SUBLANES, blk), 0)
        sub_i = lax.broadcasted_iota(jnp.int32, (sel_tiles, SUBLANES, blk), 1)
        s_i2 = lax.broadcasted_iota(jnp.int32, (C_SEL_ROWS, blk), 0)

        def pos3(i):
            return (i * C_SEL_ROWS + tile_i * SUBLANES + sub_i).astype(F32)

        need = k_eff - count(lambda kb, i: kb > thr)
        lo_i = jnp.full((1, blk), -1.0, F32)
        hi_i = jnp.zeros((1, blk), F32) + (n_sel * C_SEL_ROWS - 1).astype(F32)
        n_steps = int(math.ceil(math.log2(seq))) + 1

        def tb(_, st):
            lo_i, hi_i = st
            mid = jnp.floor((lo_i + hi_i) * 0.5)
            c = count(lambda kb, i: (kb == thr) & (pos3(i) <= mid))
            ge = c >= need
            return jnp.where(ge, lo_i, mid), jnp.where(ge, mid, hi_i)
        _, cut = lax.fori_loop(0, n_steps, tb, (lo_i, hi_i))

        def fix_body(i, carry):
            kb = keys_ref[sel_rows(i), :]
            pos = (i * C_SEL_ROWS + s_i2).astype(F32)
            sel = (kb > thr) | ((kb == thr) & (pos <= cut))
            mask_ref[sel_rows(i), :] = jnp.where(tie, jnp.where(sel, 0.0, MASK_NEG), mask_ref[sel_rows(i), :])
            return carry
        lax.fori_loop(0, n_sel, fix_body, 0)

    log2e = math.log2(math.e)
    n_far = jnp.maximum(qb - 1, 0)
    n_far_wide = lax.div(n_far, C_ATT_ROWS // blk)

    def qk_logits(rows):
        return [lax.dot_general(kv_ref[rows, kvh * C_HEAD_DIM:(kvh + 1) * C_HEAD_DIM], qg_ref[kvh], nt,
                                preferred_element_type=F32) for kvh in range(C_KV_HEADS)]

    def far_bias(h):
        return rb_ref[REL_BUCKETS - 1, h] * log2e

    def attend_online(row0, n_rows, delta, v_tile):
        rows = pl.ds(pl.multiple_of(row0, n_rows), n_rows)
        madd = mask_ref[rows, :]
        logits = qk_logits(rows)
        results = []
        for kvh in range(C_KV_HEADS):
            lg = logits[kvh]
            m_old = m_ref[kvh]
            ys, m_parts, shifts = [], [], []
            for g in range(C_GROUP):
                h = kvh * C_GROUP + g
                sl = slice(g * blk, (g + 1) * blk)
                if delta is None:
                    y = lg[:, sl] + madd
                    m_g = jnp.maximum(m_old[:, sl], jnp.max(y, axis=0, keepdims=True) + far_bias(h))
                    shifts.append(m_g - far_bias(h))
                else:
                    y = lg[:, sl] + (madd + bias_ref[h, delta] * log2e)
                    m_g = jnp.maximum(m_old[:, sl], jnp.max(y, axis=0, keepdims=True))
                    shifts.append(m_g)
                ys.append(y)
                m_parts.append(m_g)
            m_new = jnp.concatenate(m_parts, axis=1)
            alpha = jnp.exp2(m_old - m_new)
            p = jnp.concatenate([jnp.exp2(ys[g] - shifts[g]) for g in range(C_GROUP)], axis=1)
            pv = jnp.dot(v_tile(kvh), p.astype(BF16), preferred_element_type=F32)
            results.append((m_new, acc_ref[kvh] * alpha + pv))
        for kvh, (m_new, acc_new) in enumerate(results):
            m_ref[kvh] = m_new
            acc_ref[kvh] = acc_new

    def attend_direct(row0, n_rows, delta, v_tile):
        rows = pl.ds(pl.multiple_of(row0, n_rows), n_rows)
        madd = mask_ref[rows, :]
        logits = qk_logits(rows)
        results = []
        for kvh in range(C_KV_HEADS):
            lg = logits[kvh]
            parts = []
            for g in range(C_GROUP):
                h = kvh * C_GROUP + g
                sl = slice(g * blk, (g + 1) * blk)
                if delta is None:
                    parts.append(jnp.exp2(lg[:, sl] + madd))
                else:
                    parts.append(jnp.exp2(lg[:, sl] + (madd + bias_ref[h, delta] * log2e)))
            p = jnp.concatenate(parts, axis=1).astype(BF16)
            results.append(acc_ref[kvh] + jnp.dot(v_tile(kvh), p, preferred_element_type=F32))
        for kvh, acc_new in enumerate(results):
            acc_ref[kvh] = acc_new

    def apply_far_bias():
        for kvh in range(C_KV_HEADS):
            row = jnp.concatenate([jnp.exp2(jnp.full((1, blk), far_bias(kvh * C_GROUP + g), F32))
                                   for g in range(C_GROUP)], axis=1)
            acc_ref[kvh] = acc_ref[kvh] * row

    def run_attention(attend, after_far):
        acc_ref[...] = jnp.zeros_like(acc_ref)

        def attend_block(j, delta):
            attend(j * blk, blk, delta, lambda kvh: vt_ref[kvh, j])

        def far_wide_body(i, carry):
            attend(i * C_ATT_ROWS, C_ATT_ROWS, None, lambda kvh: vtw_ref[kvh, i])
            return carry
        lax.fori_loop(0, n_far_wide, far_wide_body, 0)

        def far_body(j, carry):
            attend_block(j, None)
            return carry
        lax.fori_loop(n_far_wide * (C_ATT_ROWS // blk), n_far, far_body, 0)
        after_far()

        @pl.when(qb >= 1)
        def _():
            attend_block(qb - 1, 1)

        attend_block(qb, 0)

    bound = jnp.zeros((blk, 1), F32)
    for h in range(C_HEADS):
        q_h = q_ref[:, h * C_HEAD_DIM:(h + 1) * C_HEAD_DIM].astype(F32)
        q_norm = jnp.sqrt(jnp.sum(q_h * q_h, axis=1, keepdims=True))
        bound = jnp.maximum(bound, q_norm * knorm_ref[h // C_GROUP] + bmax_ref[h])
    direct_ok = jnp.max(bound) * 1.01 < C_DIRECT_EXP_LIMIT

    @pl.when(direct_ok)
    def _():
        run_attention(attend_direct, apply_far_bias)

    @pl.when(jnp.logical_not(direct_ok))
    def _():
        m_ref[...] = jnp.full(m_ref.shape, MASK_NEG, F32)
        run_attention(attend_online, lambda: None)

    for kvh in range(C_KV_HEADS):
        o_t = acc_ref[kvh, 0:C_HEAD_DIM, :] / acc_ref[kvh, C_HEAD_DIM:C_HEAD_DIM + 1, :]
        for g in range(C_GROUP):
            h = kvh * C_GROUP + g
            o_ref[:, h * C_HEAD_DIM:(h + 1) * C_HEAD_DIM] = o_t[:, g * blk:(g + 1) * blk].T.astype(o_ref.dtype)


def _dsa(p_main, p_extra, bias_tiles, rel_bias, bsz, seq):
    assert seq % C_SEL_ROWS == 0 and seq % C_ATT_ROWS == 0
    n = p_main.shape[0]
    nb = seq // C_QBLOCK
    blk = C_QBLOCK
    ksel = min(C_TOPK_MAX, seq // 4)
    n_pair = C_IDX_HEADS // 2
    d_aug = C_HEAD_DIM + C_VPAD
    row = lambda b, i: b * nb + i
    return pl.pallas_call(
        functools.partial(_dsa_kernel, seq=seq, ksel=ksel),
        grid=(bsz, nb),
        in_specs=[
            pl.BlockSpec(memory_space=pltpu.SMEM),
            pl.BlockSpec((blk, C_WIDTH), lambda b, i: (row(b, i), 0)),
            pl.BlockSpec((seq, 2 * C_KVW), lambda b, i: (b, C_WIDTH // (2 * C_KVW))),
            pl.BlockSpec((blk, C_IDX_HEADS * C_IDX_DIM),
                         lambda b, i: (row(b, i), (C_WIDTH + 2 * C_KVW) // (C_IDX_HEADS * C_IDX_DIM))),
            pl.BlockSpec((seq, LANES), lambda b, i: (b, 0)),
            pl.BlockSpec((C_HEADS, 2, blk, blk), lambda b, i: (0, 0, 0, 0)),
        ],
        out_specs=pl.BlockSpec((blk, C_WIDTH), lambda b, i: (row(b, i), 0)),
        out_shape=jax.ShapeDtypeStruct((n, C_WIDTH), BF16),
        scratch_shapes=[
            pltpu.VMEM((seq, blk), F32),
            pltpu.VMEM((seq, blk), F32),
            pltpu.VMEM((C_KV_HEADS, nb, d_aug, blk), BF16),
            pltpu.VMEM((C_KV_HEADS, seq // C_ATT_ROWS, d_aug, C_ATT_ROWS), BF16),
            pltpu.VMEM((seq, LANES), BF16),
            pltpu.VMEM((seq, LANES), BF16),
            pltpu.VMEM((n_pair * blk, LANES), BF16),
            pltpu.VMEM((C_KV_HEADS, C_GROUP * blk, C_HEAD_DIM), BF16),
            pltpu.VMEM((C_KV_HEADS, d_aug, C_GROUP * blk), F32),
            pltpu.VMEM((C_KV_HEADS, 1, C_GROUP * blk), F32),
            pltpu.SMEM((C_KV_HEADS,), F32),
            pltpu.SMEM((C_HEADS,), F32),
        ],
        compiler_params=_params("parallel", "arbitrary"),
        name="dsa",
    )(rel_bias, p_main, p_main, p_main, p_extra, bias_tiles)


def _pad_cols(w, width):
    return jnp.pad(w, ((0, 0), (0, width - w.shape[1])))


def _even_mixer(h, g, w_in, v_ln_g, w_s, b_s, gate_w2, gate_b, out_norm_g, w_out, bsz, seq):
    n_main = 2 * A_WIDTH + 2 * B_HEADS * B_DK + 2 * B_WIDTH
    w_main = w_in[:, :n_main].astype(BF16)
    w_extra = _pad_cols(w_in[:, n_main:], LANES).astype(BF16)
    p_main, p_extra = _norm_matmul(h, g, w_main, w_extra, F32)
    a_out = _sgu(p_main, v_ln_g, w_s, b_s.T)
    gate_w2_pad = jnp.pad(gate_w2, ((0, LANES - B_RANK), (0, 0))).astype(BF16)
    b_out = _gla(p_main, p_extra, gate_w2_pad, gate_b, out_norm_g, bsz, seq)
    w_out = w_out.astype(BF16)
    return _proj_residual(h, [a_out, b_out], [w_out[:A_WIDTH], w_out[A_WIDTH:]])


def _odd_mixer(h, g, w_in, w_out, bias_tiles, rel_bias, bsz, seq):
    n_main = C_WIDTH + 2 * C_KVW + C_IDX_HEADS * C_IDX_DIM
    q_scale = (C_HEAD_DIM ** -0.5) * math.log2(math.e)
    col_scale = jnp.where(jnp.arange(n_main) < C_WIDTH, q_scale, 1.0).astype(F32)
    w_main = (w_in[:, :n_main] * col_scale).astype(BF16)
    w_extra = _pad_cols(w_in[:, n_main:], LANES).astype(BF16)
    p_main, p_extra = _norm_matmul(h, g, w_main, w_extra, BF16)
    o = _dsa(p_main, p_extra, bias_tiles, rel_bias, bsz, seq)
    return _proj_residual(h, [o], [w_out.astype(BF16)])


def kernel(x, norm_mix_g, norm_ffn_g, final_norm_g, ab_w_in, a_v_ln_g, a_w_s, a_b_s, b_gate_w2,
           b_gate_b, b_out_norm_g, ab_w_out, c_w_in, c_w_out, rel_bias, ffn_w1, ffn_w2):
    bsz, seq, d = x.shape
    depth = norm_mix_g.shape[0]
    h = x.reshape(bsz * seq, d)
    bias_tiles = _bias_tiles(rel_bias)
    for layer in range(depth):
        i = layer // 2
        if layer % 2 == 0:
            h = _even_mixer(h, norm_mix_g[layer], ab_w_in[i], a_v_ln_g[i], a_w_s[i], a_b_s[i],
                            b_gate_w2[i], b_gate_b[i], b_out_norm_g[i], ab_w_out[i], bsz, seq)
        else:
            h = _odd_mixer(h, norm_mix_g[layer], c_w_in[i], c_w_out[i], bias_tiles, rel_bias, bsz, seq)
        h = _ffn(h, norm_ffn_g[layer], ffn_w1[layer].astype(BF16), ffn_w2[layer].astype(BF16),
                 final_norm_g, layer == depth - 1)
    return h.reshape(bsz, seq, d)
```

```python
import functools
import math

import jax
import jax.numpy as jnp
from jax import lax
from jax.experimental import pallas as pl
from jax.experimental.pallas import tpu as pltpu

F32 = jnp.float32
BF16 = jnp.bfloat16
EPS = 1e-6

VMEM_LIMIT_BYTES = 56 * 1024 * 1024
LANES = 128
SUBLANES = 8

A_GROUPS = 8
A_DIM = 128
A_CHUNK = 128
A_WIDTH = A_GROUPS * A_DIM
B_HEADS = 4
B_DK = 128
B_DV = 256
B_RANK = 16
B_TAU = 16.0
B_CHUNK = 64
B_WIDTH = B_HEADS * B_DV
C_HEADS = 16
C_KV_HEADS = 4
C_GROUP = C_HEADS // C_KV_HEADS
C_HEAD_DIM = 128
C_IDX_HEADS = 16
C_IDX_DIM = 64
C_TOPK_MAX = 256
C_QBLOCK = 128
C_WIDTH = C_HEADS * C_HEAD_DIM
C_KVW = C_KV_HEADS * C_HEAD_DIM
C_SEL_ROWS = 4 * C_QBLOCK
C_IDX_ROWS = 4 * C_QBLOCK
C_ATT_ROWS = 4 * C_QBLOCK
C_VPAD = 16
C_BISECT_WARMUP = 14
C_DIRECT_EXP_LIMIT = 60.0
REL_BUCKETS = 32
REL_MAX_DIST = 128
MASK_NEG = -1e30


def _params(*sem):
    return pltpu.CompilerParams(dimension_semantics=sem, vmem_limit_bytes=VMEM_LIMIT_BYTES)


def _rms_rows(x, g):
    ms = jnp.mean(x * x, axis=-1, keepdims=True)
    return x * lax.rsqrt(ms + EPS) * g


def _norm_matmul_kernel(x_ref, g_ref, w_ref, we_ref, o_ref, oe_ref, xn_ref):
    @pl.when(pl.program_id(1) == 0)
    def _():
        xn = _rms_rows(x_ref[...], g_ref[...]).astype(BF16)
        xn_ref[...] = xn
        oe_ref[...] = jnp.dot(xn, we_ref[...], preferred_element_type=F32)

    o_ref[...] = jnp.dot(xn_ref[...], w_ref[...], preferred_element_type=F32).astype(o_ref.dtype)


def _norm_matmul(x, g, w_main, w_extra, out_dtype, tm=1024, tn=1024):
    n, d = x.shape
    nm = w_main.shape[1]
    return pl.pallas_call(
        _norm_matmul_kernel,
        grid=(n // tm, nm // tn),
        in_specs=[
            pl.BlockSpec((tm, d), lambda i, j: (i, 0)),
            pl.BlockSpec((1, d), lambda i, j: (0, 0)),
            pl.BlockSpec((d, tn), lambda i, j: (0, j)),
            pl.BlockSpec((d, LANES), lambda i, j: (0, 0)),
        ],
        out_specs=[
            pl.BlockSpec((tm, tn), lambda i, j: (i, j)),
            pl.BlockSpec((tm, LANES), lambda i, j: (i, 0)),
        ],
        out_shape=[jax.ShapeDtypeStruct((n, nm), out_dtype), jax.ShapeDtypeStruct((n, LANES), F32)],
        scratch_shapes=[pltpu.VMEM((tm, d), BF16)],
        compiler_params=_params("parallel", "arbitrary"),
        name="norm_matmul",
    )(x, g.reshape(1, d), w_main, w_extra)


def _proj_res_kernel(*refs, n_in):
    h_ref = refs[0]
    o_ref = refs[1 + 2 * n_in]
    acc = h_ref[...]
    for k in range(n_in):
        acc = acc + jnp.dot(refs[1 + k][...], refs[1 + n_in + k][...], preferred_element_type=F32)
    o_ref[...] = acc


def _proj_residual(h, xs, ws, tm=512, tn=2048):
    n, d = h.shape
    n_in = len(xs)
    in_specs = [pl.BlockSpec((tm, tn), lambda i, j: (i, j))]
    in_specs += [pl.BlockSpec((tm, x.shape[1]), lambda i, j: (i, 0)) for x in xs]
    in_specs += [pl.BlockSpec((w.shape[0], tn), lambda i, j: (0, j)) for w in ws]
    return pl.pallas_call(
        functools.partial(_proj_res_kernel, n_in=n_in),
        grid=(n // tm, d // tn),
        in_specs=in_specs,
        out_specs=pl.BlockSpec((tm, tn), lambda i, j: (i, j)),
        out_shape=jax.ShapeDtypeStruct((n, d), F32),
        compiler_params=_params("parallel", "parallel"),
        name="proj_residual",
    )(h, *xs, *ws)


def _ffn_kernel(x_ref, g_ref, w1_ref, w2_ref, fg_ref, o_ref, xn_ref, acc_ref, *, final_norm):
    f = pl.program_id(1)

    @pl.when(f == 0)
    def _():
        xn_ref[...] = _rms_rows(x_ref[...], g_ref[...]).astype(BF16)
        acc_ref[...] = jnp.zeros_like(acc_ref)

    h1 = jnp.dot(xn_ref[...], w1_ref[...], preferred_element_type=F32)
    h1 = jnp.square(jnp.maximum(h1, 0.0)).astype(BF16)
    acc_ref[...] += jnp.dot(h1, w2_ref[...], preferred_element_type=F32)

    @pl.when(f == pl.num_programs(1) - 1)
    def _():
        y = x_ref[...] + acc_ref[...]
        if final_norm:
            y = _rms_rows(y, fg_ref[...])
        o_ref[...] = y


def _ffn(x, g, w1, w2, final_g, final_norm, tm=512, tf=1024):
    n, d = x.shape
    dff = w1.shape[1]
    return pl.pallas_call(
        functools.partial(_ffn_kernel, final_norm=final_norm),
        grid=(n // tm, dff // tf),
        in_specs=[
            pl.BlockSpec((tm, d), lambda i, f: (i, 0)),
            pl.BlockSpec((1, d), lambda i, f: (0, 0)),
            pl.BlockSpec((d, tf), lambda i, f: (0, f)),
            pl.BlockSpec((tf, d), lambda i, f: (f, 0)),
            pl.BlockSpec((1, d), lambda i, f: (0, 0)),
        ],
        out_specs=pl.BlockSpec((tm, d), lambda i, f: (i, 0)),
        out_shape=jax.ShapeDtypeStruct((n, d), F32),
        scratch_shapes=[pltpu.VMEM((tm, d), BF16), pltpu.VMEM((tm, d), F32)],
        compiler_params=_params("parallel", "arbitrary"),
        name="ffn",
    )(x, g.reshape(1, d), w1, w2, final_g.reshape(1, d))


def _gelu(x):
    return 0.5 * x * (1.0 + lax.erf(x * math.sqrt(0.5)))


def _sgu_kernel(u_ref, v_ref, lng_ref, w_ref, bt_ref, o_ref):
    t_i = lax.broadcasted_iota(jnp.int32, (A_CHUNK, A_CHUNK), 0)
    s_i = lax.broadcasted_iota(jnp.int32, (A_CHUNK, A_CHUNK), 1)
    causal = s_i <= t_i
    n_sub = u_ref.shape[0] // A_CHUNK
    for g in range(A_GROUPS):
        sl = slice(g * A_DIM, (g + 1) * A_DIM)
        w = jnp.where(causal, w_ref[g], 0.0).astype(BF16)
        for sub in range(n_sub):
            rsl = slice(sub * A_CHUNK, (sub + 1) * A_CHUNK)
            v = _gelu(v_ref[rsl, sl])
            mu = jnp.mean(v, axis=-1, keepdims=True)
            vc = v - mu
            var = jnp.mean(vc * vc, axis=-1, keepdims=True)
            vn = vc * lax.rsqrt(var + EPS) * lng_ref[g:g + 1, :]
            z = jnp.dot(w, vn.astype(BF16), preferred_element_type=F32) + bt_ref[:, g:g + 1]
            o_ref[rsl, sl] = (_gelu(u_ref[rsl, sl]) * z).astype(o_ref.dtype)


def _sgu(p_main, ln_g, w_s, b_s_t, chunks_per_step=4):
    n = p_main.shape[0]
    rows = A_CHUNK * chunks_per_step
    return pl.pallas_call(
        _sgu_kernel,
        grid=(n // rows,),
        in_specs=[
            pl.BlockSpec((rows, A_WIDTH), lambda i: (i, 0)),
            pl.BlockSpec((rows, A_WIDTH), lambda i: (i, 1)),
            pl.BlockSpec((A_GROUPS, A_DIM), lambda i: (0, 0)),
            pl.BlockSpec((A_GROUPS, A_CHUNK, A_CHUNK), lambda i: (0, 0, 0)),
            pl.BlockSpec((A_CHUNK, A_GROUPS), lambda i: (0, 0)),
        ],
        out_specs=pl.BlockSpec((rows, A_WIDTH), lambda i: (i, 0)),
        out_shape=jax.ShapeDtypeStruct((n, A_WIDTH), BF16),
        compiler_params=_params("parallel"),
        name="sgu",
    )(p_main, p_main, ln_g, w_s, b_s_t)


def _cumsum_rows(x):
    n = x.shape[0]
    row = lax.broadcasted_iota(jnp.int32, x.shape, 0)
    sh = 1
    while sh < n:
        x = x + jnp.where(row >= sh, pltpu.roll(x, sh, axis=0), 0.0)
        sh *= 2
    return x


def _log_sigmoid(x):
    return jnp.minimum(x, 0.0) - jnp.log1p(jnp.exp(-jnp.abs(x)))


def _gla_kernel(q_ref, k_ref, v_ref, r_ref, g_ref, w2_ref, gb_ref, ng_ref, o_ref, s_ref):
    @pl.when(pl.program_id(1) == 0)
    def _():
        s_ref[...] = jnp.zeros_like(s_ref)

    c = B_CHUNK
    n_sub = q_ref.shape[0] // c
    gate = jnp.dot(g_ref[...].astype(BF16), w2_ref[...], preferred_element_type=F32) + gb_ref[...]
    log_a = _log_sigmoid(gate) * (1.0 / B_TAU)
    i_i = lax.broadcasted_iota(jnp.int32, (c, c), 0)
    j_i = lax.broadcasted_iota(jnp.int32, (c, c), 1)
    causal = j_i <= i_i
    for h in range(B_HEADS):
        ksl = slice(h * B_DK, (h + 1) * B_DK)
        vsl = slice(h * B_DV, (h + 1) * B_DV)
        state = s_ref[h]
        for sub in range(n_sub):
            rsl = slice(sub * c, (sub + 1) * c)
            cum = _cumsum_rows(log_a[rsl, ksl])
            last = cum[c - 1:c, :]
            ref = 0.5 * last
            q = q_ref[rsl, ksl] * (B_DK ** -0.5)
            k = k_ref[rsl, ksl]
            v = v_ref[rsl, vsl].astype(BF16)
            qe = (q * jnp.exp(cum - ref)).astype(BF16)
            ke = (k * jnp.exp(ref - cum)).astype(BF16)
            scores = lax.dot_general(qe, ke, (((1,), (1,)), ((), ())), preferred_element_type=F32)
            scores = jnp.where(causal, scores, 0.0).astype(BF16)
            o = jnp.dot(scores, v, preferred_element_type=F32)
            q_inter = (q * jnp.exp(cum)).astype(BF16)
            o = o + jnp.dot(q_inter, state.astype(BF16), preferred_element_type=F32)
            cum_t = cum.T
            last_t = cum_t[:, c - 1:c]
            k_state_t = (k.T * jnp.exp(last_t - cum_t)).astype(BF16)
            state = jnp.exp(last_t) * state + jnp.dot(k_state_t, v, preferred_element_type=F32)
            o = _rms_rows(o, ng_ref[...])
            r = r_ref[rsl, vsl]
            o_ref[rsl, vsl] = (o * (r * jax.nn.sigmoid(r))).astype(o_ref.dtype)
        s_ref[h] = state


def _gla(p_main, p_extra, gate_w2_pad, gate_b, norm_g, bsz, seq, chunks_per_step=4):
    n = p_main.shape[0]
    c = B_CHUNK * chunks_per_step
    nc = seq // c
    qk_w = B_HEADS * B_DK
    row = lambda b, i: b * nc + i
    return pl.pallas_call(
        _gla_kernel,
        grid=(bsz, nc),
        in_specs=[
            pl.BlockSpec((c, qk_w), lambda b, i: (row(b, i), (2 * A_WIDTH) // qk_w)),
            pl.BlockSpec((c, qk_w), lambda b, i: (row(b, i), (2 * A_WIDTH) // qk_w + 1)),
            pl.BlockSpec((c, B_WIDTH), lambda b, i: (row(b, i), (2 * A_WIDTH + 2 * qk_w) // B_WIDTH)),
            pl.BlockSpec((c, B_WIDTH), lambda b, i: (row(b, i), (2 * A_WIDTH + 2 * qk_w) // B_WIDTH + 1)),
            pl.BlockSpec((c, LANES), lambda b, i: (row(b, i), 0)),
            pl.BlockSpec((LANES, qk_w), lambda b, i: (0, 0)),
            pl.BlockSpec((1, qk_w), lambda b, i: (0, 0)),
            pl.BlockSpec((1, B_DV), lambda b, i: (0, 0)),
        ],
        out_specs=pl.BlockSpec((c, B_WIDTH), lambda b, i: (row(b, i), 0)),
        out_shape=jax.ShapeDtypeStruct((n, B_WIDTH), BF16),
        scratch_shapes=[pltpu.VMEM((B_HEADS, B_DK, B_DV), F32)],
        compiler_params=_params("parallel", "arbitrary"),
        name="gla",
    )(p_main, p_main, p_main, p_main, p_extra, gate_w2_pad, gate_b.reshape(1, qk_w),
      norm_g.reshape(1, B_DV))


def _t5_bucket(dist):
    max_exact = REL_BUCKETS // 2
    d = jnp.maximum(dist, 1).astype(F32)
    large = max_exact + (jnp.log(d / max_exact) / math.log(REL_MAX_DIST / max_exact)
                         * (REL_BUCKETS - max_exact)).astype(jnp.int32)
    large = jnp.minimum(large, REL_BUCKETS - 1)
    return jnp.where(dist < max_exact, dist, large)


def _bias_kernel(rb_ref, o_ref):
    s_i = lax.broadcasted_iota(jnp.int32, (C_QBLOCK, C_QBLOCK), 0)
    t_i = lax.broadcasted_iota(jnp.int32, (C_QBLOCK, C_QBLOCK), 1)
    for delta in range(2):
        bucket = _t5_bucket(jnp.maximum(delta * C_QBLOCK + t_i - s_i, 0))
        for h in range(C_HEADS):
            acc = jnp.zeros((C_QBLOCK, C_QBLOCK), F32)
            for b in range(REL_BUCKETS):
                acc = jnp.where(bucket == b, rb_ref[b, h], acc)
            o_ref[h, delta] = acc


def _bias_tiles(rel_bias):
    return pl.pallas_call(
        _bias_kernel,
        in_specs=[pl.BlockSpec(memory_space=pltpu.SMEM)],
        out_specs=pl.BlockSpec(memory_space=pltpu.VMEM),
        out_shape=jax.ShapeDtypeStruct((C_HEADS, 2, C_QBLOCK, C_QBLOCK), F32),
        name="bias_tiles",
    )(rel_bias)


def _dsa_kernel(rb_ref, q_ref, kv_ref, iq_ref, ex_ref, bias_ref, hsel_ref, o_ref,
                keys_ref, mask_ref, vt_ref, vtw_ref, iklo_ref, ikhi_ref, iqa_ref, qg_ref,
                acc_ref, m_ref, bvec_ref, *, seq, ksel):
    qb = pl.program_id(1)
    blk = C_QBLOCK
    nb = seq // blk
    n_pair = C_IDX_HEADS // 2
    n_sel = lax.shift_right_logical(qb, 2) + 1
    sel_tiles = C_SEL_ROWS // SUBLANES
    lane = lax.broadcasted_iota(jnp.int32, (blk, LANES), 1)
    t_row = lax.broadcasted_iota(jnp.int32, (1, blk), 1)
    d_aug = C_HEAD_DIM + C_VPAD

    def ones_row(width):
        r = lax.broadcasted_iota(jnp.int32, (C_VPAD, width), 0)
        return jnp.where(r == 0, 1.0, 0.0).astype(BF16)

    @pl.when(qb == 0)
    def _():
        def fill(j, k_sq):
            rows = pl.ds(pl.multiple_of(j * blk, blk), blk)
            ex = ex_ref[rows, :]
            iklo_ref[rows, :] = jnp.where(lane < C_IDX_DIM, ex, 0.0).astype(BF16)
            ikhi_ref[rows, :] = jnp.where(lane >= C_IDX_DIM, pltpu.roll(ex, C_IDX_DIM, axis=1),
                                          0.0).astype(BF16)
            for kvh in range(C_KV_HEADS):
                vblk = kv_ref[rows, C_KVW + kvh * C_HEAD_DIM:C_KVW + (kvh + 1) * C_HEAD_DIM]
                vt_ref[kvh, j, 0:C_HEAD_DIM, :] = vblk.astype(F32).T.astype(BF16)
                vt_ref[kvh, j, C_HEAD_DIM:d_aug, :] = ones_row(blk)
            k_sq_new = []
            for kvh in range(C_KV_HEADS):
                kblk = kv_ref[rows, kvh * C_HEAD_DIM:(kvh + 1) * C_HEAD_DIM].astype(F32)
                k_sq_new.append(jnp.maximum(k_sq[kvh], jnp.sum(kblk * kblk, axis=1, keepdims=True)))
            return tuple(k_sq_new)
        k_sq = lax.fori_loop(0, nb, fill, tuple(jnp.zeros((blk, 1), F32) for _ in range(C_KV_HEADS)))
        lane_row = lax.broadcasted_iota(jnp.int32, (1, LANES), 1)
        kn_row = jnp.zeros((1, LANES), F32)
        b_row = jnp.zeros((1, LANES), F32)
        for kvh in range(C_KV_HEADS):
            kn_row = jnp.where(lax.div(lane_row, C_GROUP) == kvh, jnp.max(jnp.sqrt(k_sq[kvh])), kn_row)
        for h in range(C_HEADS):
            b_abs = jnp.abs(rb_ref[0, h])
            for b in range(1, REL_BUCKETS):
                b_abs = jnp.maximum(b_abs, jnp.abs(rb_ref[b, h]))
            b_row = jnp.where(lane_row == h, b_abs * math.log2(math.e), b_row)
        bvec_ref[0:1, :] = kn_row
        bvec_ref[1:2, :] = b_row

        def fill_wide(i, carry):
            rows = pl.ds(pl.multiple_of(i * C_ATT_ROWS, C_ATT_ROWS), C_ATT_ROWS)
            for kvh in range(C_KV_HEADS):
                vblk = kv_ref[rows, C_KVW + kvh * C_HEAD_DIM:C_KVW + (kvh + 1) * C_HEAD_DIM]
                vtw_ref[kvh, i, 0:C_HEAD_DIM, :] = vblk.astype(F32).T.astype(BF16)
                vtw_ref[kvh, i, C_HEAD_DIM:d_aug, :] = ones_row(C_ATT_ROWS)
            return carry
        lax.fori_loop(0, seq // C_ATT_ROWS, fill_wide, 0)

    q_rows = pl.ds(pl.multiple_of(qb * blk, blk), blk)
    ext_t = ex_ref[q_rows, :].T
    idx_scale = (C_IDX_DIM ** -0.5) * (C_IDX_HEADS ** -0.5)
    w_idx = [ext_t[C_IDX_DIM + h:C_IDX_DIM + h + 1, :] * idx_scale for h in range(C_IDX_HEADS)]
    for p in range(n_pair):
        iqa_ref[p * blk:(p + 1) * blk, :] = iq_ref[:, p * LANES:(p + 1) * LANES]
    for kvh in range(C_KV_HEADS):
        for g in range(C_GROUP):
            h = kvh * C_GROUP + g
            qg_ref[kvh, g * blk:(g + 1) * blk, :] = q_ref[:, h * C_HEAD_DIM:(h + 1) * C_HEAD_DIM]

    s_iu = lax.broadcasted_iota(jnp.int32, (C_IDX_ROWS, blk), 0)
    t_iu = lax.broadcasted_iota(jnp.int32, (C_IDX_ROWS, blk), 1)
    nt = (((1,), (1,)), ((), ()))

    def idx_body(u, carry):
        mx, mn = carry
        rows = pl.ds(pl.multiple_of(u * C_IDX_ROWS, C_IDX_ROWS), C_IDX_ROWS)
        ik_lo = iklo_ref[rows, :]
        ik_hi = ikhi_ref[rows, :]
        acc = jnp.zeros((C_IDX_ROWS, blk), F32)
        for c in range(n_pair // 2):
            w = iqa_ref[c * 2 * blk:(c + 1) * 2 * blk, :]
            sc_even = lax.dot_general(ik_lo, w, nt, preferred_element_type=F32)
            sc_odd = lax.dot_general(ik_hi, w, nt, preferred_element_type=F32)
            for pp in range(2):
                p = 2 * c + pp
                sl = slice(pp * blk, (pp + 1) * blk)
                acc = acc + jnp.maximum(sc_even[:, sl], 0.0) * w_idx[2 * p]
                acc = acc + jnp.maximum(sc_odd[:, sl], 0.0) * w_idx[2 * p + 1]
        causal = (u * C_IDX_ROWS + s_iu) <= (qb * blk + t_iu)
        keys_ref[rows, :] = jnp.where(causal, acc, -jnp.inf)
        mx = jnp.maximum(mx, jnp.max(jnp.where(causal, acc, -jnp.inf), axis=0, keepdims=True))
        mn = jnp.minimum(mn, jnp.min(jnp.where(causal, acc, jnp.inf), axis=0, keepdims=True))
        return mx, mn

    mx, mn = lax.fori_loop(0, n_sel * (C_SEL_ROWS // C_IDX_ROWS), idx_body,
                           (jnp.full((1, blk), -jnp.inf, F32), jnp.full((1, blk), jnp.inf, F32)))

    def sel_rows(i):
        return pl.ds(pl.multiple_of(i * C_SEL_ROWS, C_SEL_ROWS), C_SEL_ROWS)

    def tree_sum(x):
        parts = [x[i] for i in range(x.shape[0])]
        while len(parts) > 1:
            parts = [parts[i] + parts[i + 1] for i in range(0, len(parts), 2)]
        return parts[0]

    def count(pred):
        def body(i, c):
            kb = keys_ref[sel_rows(i), :].reshape(sel_tiles, SUBLANES, blk)
            return c + tree_sum(jnp.where(pred(kb, i), 1.0, 0.0))
        c = lax.fori_loop(0, n_sel, body, jnp.zeros((SUBLANES, blk), F32))
        return jnp.sum(c, axis=0, keepdims=True)

    n_causal = (qb * blk + t_row + 1).astype(F32)
    k_eff = jnp.minimum(n_causal, float(ksel))
    lo0 = mn
    hi0 = mx + (jnp.abs(mx) * 1e-6 + 1e-30)
    cnt0 = n_causal

    def active_of(lo, hi, cnt):
        mid = lo + 0.5 * (hi - lo)
        collapsed = (mid <= lo) | (mid >= hi)
        return mid, (cnt > k_eff) & jnp.logical_not(collapsed)

    def bis_cond(st):
        return st[3] > 0.0

    def bis_step(_, st):
        lo, hi, cnt = st
        mid, active = active_of(lo, hi, cnt)
        c = count(lambda kb, i: kb >= mid)
        ge = c >= k_eff
        return (jnp.where(active & ge, mid, lo), jnp.where(active & jnp.logical_not(ge), mid, hi),
                jnp.where(active & ge, c, cnt))

    def n_active(st):
        _, active = active_of(*st)
        return jnp.sum(jnp.where(active, 1.0, 0.0))

    def bis_body(st):
        st = lax.fori_loop(0, 2, bis_step, st[:3])
        return (*st, n_active(st))

    n_warm = jnp.where(qb * blk >= ksel, C_BISECT_WARMUP, 0)
    st = lax.fori_loop(0, n_warm, bis_step, (lo0, hi0, cnt0))
    thr, _, cnt, _ = lax.while_loop(bis_cond, bis_body, (*st, n_active(st)))

    def mask_body(i, carry):
        mask_ref[sel_rows(i), :] = jnp.where(keys_ref[sel_rows(i), :] >= thr, 0.0, MASK_NEG)
        return carry
    lax.fori_loop(0, n_sel, mask_body, 0)

    tie = cnt > k_eff

    @pl.when(jnp.sum(jnp.where(tie, 1.0, 0.0)) > 0.0)
    def _():
        tile_i = lax.broadcasted_iota(jnp.int32, (sel_tiles, SUBLANES, blk), 0)
        sub_i = lax.broadcasted_iota(jnp.int32, (sel_tiles, SUBLANES, blk), 1)
        s_i2 = lax.broadcasted_iota(jnp.int32, (C_SEL_ROWS, blk), 0)

        def pos3(i):
            return (i * C_SEL_ROWS + tile_i * SUBLANES + sub_i).astype(F32)

        need = k_eff - count(lambda kb, i: kb > thr)
        lo_i = jnp.full((1, blk), -1.0, F32)
        hi_i = jnp.zeros((1, blk), F32) + (n_sel * C_SEL_ROWS - 1).astype(F32)
        n_steps = int(math.ceil(math.log2(seq))) + 1

        def tb(_, st):
            lo_i, hi_i = st
            mid = jnp.floor((lo_i + hi_i) * 0.5)
            c = count(lambda kb, i: (kb == thr) & (pos3(i) <= mid))
            ge = c >= need
            return jnp.where(ge, lo_i, mid), jnp.where(ge, mid, hi_i)
        _, cut = lax.fori_loop(0, n_steps, tb, (lo_i, hi_i))

        def fix_body(i, carry):
            kb = keys_ref[sel_rows(i), :]
            pos = (i * C_SEL_ROWS + s_i2).astype(F32)
            sel = (kb > thr) | ((kb == thr) & (pos <= cut))
            mask_ref[sel_rows(i), :] = jnp.where(tie, jnp.where(sel, 0.0, MASK_NEG), mask_ref[sel_rows(i), :])
            return carry
        lax.fori_loop(0, n_sel, fix_body, 0)

    log2e = math.log2(math.e)
    n_far = jnp.maximum(qb - 1, 0)
    n_far_wide = lax.div(n_far, C_ATT_ROWS // blk)

    def qk_logits(rows):
        return [lax.dot_general(kv_ref[rows, kvh * C_HEAD_DIM:(kvh + 1) * C_HEAD_DIM], qg_ref[kvh], nt,
                                preferred_element_type=F32) for kvh in range(C_KV_HEADS)]

    def far_bias(h):
        return rb_ref[REL_BUCKETS - 1, h] * log2e

    def attend_online(row0, n_rows, delta, v_tile):
        rows = pl.ds(pl.multiple_of(row0, n_rows), n_rows)
        madd = mask_ref[rows, :]
        logits = qk_logits(rows)
        results = []
        for kvh in range(C_KV_HEADS):
            lg = logits[kvh]
            m_old = m_ref[kvh]
            ys, m_parts, shifts = [], [], []
            for g in range(C_GROUP):
                h = kvh * C_GROUP + g
                sl = slice(g * blk, (g + 1) * blk)
                if delta is None:
                    y = lg[:, sl] + madd
                    m_g = jnp.maximum(m_old[:, sl], jnp.max(y, axis=0, keepdims=True) + far_bias(h))
                    shifts.append(m_g - far_bias(h))
                else:
                    y = lg[:, sl] + (madd + bias_ref[h, delta] * log2e)
                    m_g = jnp.maximum(m_old[:, sl], jnp.max(y, axis=0, keepdims=True))
                    shifts.append(m_g)
                ys.append(y)
                m_parts.append(m_g)
            m_new = jnp.concatenate(m_parts, axis=1)
            alpha = jnp.exp2(m_old - m_new)
            p = jnp.concatenate([jnp.exp2(ys[g] - shifts[g]) for g in range(C_GROUP)], axis=1)
            pv = jnp.dot(v_tile(kvh), p.astype(BF16), preferred_element_type=F32)
            results.append((m_new, acc_ref[kvh] * alpha + pv))
        for kvh, (m_new, acc_new) in enumerate(results):
            m_ref[kvh] = m_new
            acc_ref[kvh] = acc_new

    def attend_direct(row0, n_rows, delta, v_tile):
        rows = pl.ds(pl.multiple_of(row0, n_rows), n_rows)
        madd = mask_ref[rows, :]
        logits = qk_logits(rows)
        results = []
        for kvh in range(C_KV_HEADS):
            lg = logits[kvh]
            parts = []
            for g in range(C_GROUP):
                h = kvh * C_GROUP + g
                sl = slice(g * blk, (g + 1) * blk)
                if delta is None:
                    parts.append(jnp.exp2(lg[:, sl] + madd))
                else:
                    parts.append(jnp.exp2(lg[:, sl] + (madd + bias_ref[h, delta] * log2e)))
            p = jnp.concatenate(parts, axis=1).astype(BF16)
            results.append(acc_ref[kvh] + jnp.dot(v_tile(kvh), p, preferred_element_type=F32))
        for kvh, acc_new in enumerate(results):
            acc_ref[kvh] = acc_new

    def apply_far_bias():
        for kvh in range(C_KV_HEADS):
            row = jnp.concatenate([jnp.exp2(jnp.full((1, blk), far_bias(kvh * C_GROUP + g), F32))
                                   for g in range(C_GROUP)], axis=1)
            acc_ref[kvh] = acc_ref[kvh] * row

    def run_attention(attend, after_far):
        acc_ref[...] = jnp.zeros_like(acc_ref)

        def attend_block(j, delta):
            attend(j * blk, blk, delta, lambda kvh: vt_ref[kvh, j])

        def far_wide_body(i, carry):
            attend(i * C_ATT_ROWS, C_ATT_ROWS, None, lambda kvh: vtw_ref[kvh, i])
            return carry
        lax.fori_loop(0, n_far_wide, far_wide_body, 0)

        def far_body(j, carry):
            attend_block(j, None)
            return carry
        lax.fori_loop(n_far_wide * (C_ATT_ROWS // blk), n_far, far_body, 0)
        after_far()

        @pl.when(qb >= 1)
        def _():
            attend_block(qb - 1, 1)

        attend_block(qb, 0)

    q_all = q_ref[...].astype(F32)
    q_sq = jnp.dot((q_all * q_all).astype(BF16), hsel_ref[...], preferred_element_type=F32)
    bound = jnp.sqrt(q_sq) * bvec_ref[0:1, :] + bvec_ref[1:2, :]
    direct_ok = jnp.max(bound) * 1.05 < C_DIRECT_EXP_LIMIT

    @pl.when(direct_ok)
    def _():
        run_attention(attend_direct, apply_far_bias)

    @pl.when(jnp.logical_not(direct_ok))
    def _():
        m_ref[...] = jnp.full(m_ref.shape, MASK_NEG, F32)
        run_attention(attend_online, lambda: None)

    for kvh in range(C_KV_HEADS):
        o_t = acc_ref[kvh, 0:C_HEAD_DIM, :] / acc_ref[kvh, C_HEAD_DIM:C_HEAD_DIM + 1, :]
        for g in range(C_GROUP):
            h = kvh * C_GROUP + g
            o_ref[:, h * C_HEAD_DIM:(h + 1) * C_HEAD_DIM] = o_t[:, g * blk:(g + 1) * blk].T.astype(o_ref.dtype)


def _dsa(p_main, p_extra, bias_tiles, rel_bias, bsz, seq):
    assert seq % C_SEL_ROWS == 0 and seq % C_ATT_ROWS == 0
    n = p_main.shape[0]
    nb = seq // C_QBLOCK
    blk = C_QBLOCK
    ksel = min(C_TOPK_MAX, seq // 4)
    n_pair = C_IDX_HEADS // 2
    d_aug = C_HEAD_DIM + C_VPAD
    head_sel = jnp.repeat(jnp.eye(C_HEADS, LANES, dtype=BF16), C_HEAD_DIM, axis=0)
    row = lambda b, i: b * nb + i
    return pl.pallas_call(
        functools.partial(_dsa_kernel, seq=seq, ksel=ksel),
        grid=(bsz, nb),
        in_specs=[
            pl.BlockSpec(memory_space=pltpu.SMEM),
            pl.BlockSpec((blk, C_WIDTH), lambda b, i: (row(b, i), 0)),
            pl.BlockSpec((seq, 2 * C_KVW), lambda b, i: (b, C_WIDTH // (2 * C_KVW))),
            pl.BlockSpec((blk, C_IDX_HEADS * C_IDX_DIM),
                         lambda b, i: (row(b, i), (C_WIDTH + 2 * C_KVW) // (C_IDX_HEADS * C_IDX_DIM))),
            pl.BlockSpec((seq, LANES), lambda b, i: (b, 0)),
            pl.BlockSpec((C_HEADS, 2, blk, blk), lambda b, i: (0, 0, 0, 0)),
            pl.BlockSpec((C_WIDTH, LANES), lambda b, i: (0, 0)),
        ],
        out_specs=pl.BlockSpec((blk, C_WIDTH), lambda b, i: (row(b, i), 0)),
        out_shape=jax.ShapeDtypeStruct((n, C_WIDTH), BF16),
        scratch_shapes=[
            pltpu.VMEM((seq, blk), F32),
            pltpu.VMEM((seq, blk), F32),
            pltpu.VMEM((C_KV_HEADS, nb, d_aug, blk), BF16),
            pltpu.VMEM((C_KV_HEADS, seq // C_ATT_ROWS, d_aug, C_ATT_ROWS), BF16),
            pltpu.VMEM((seq, LANES), BF16),
            pltpu.VMEM((seq, LANES), BF16),
            pltpu.VMEM((n_pair * blk, LANES), BF16),
            pltpu.VMEM((C_KV_HEADS, C_GROUP * blk, C_HEAD_DIM), BF16),
            pltpu.VMEM((C_KV_HEADS, d_aug, C_GROUP * blk), F32),
            pltpu.VMEM((C_KV_HEADS, 1, C_GROUP * blk), F32),
            pltpu.VMEM((SUBLANES, LANES), F32),
        ],
        compiler_params=_params("parallel", "arbitrary"),
        name="dsa",
    )(rel_bias, p_main, p_main, p_main, p_extra, bias_tiles, head_sel)


def _pad_cols(w, width):
    return jnp.pad(w, ((0, 0), (0, width - w.shape[1])))


def _even_mixer(h, g, w_in, v_ln_g, w_s, b_s, gate_w2, gate_b, out_norm_g, w_out, bsz, seq):
    n_main = 2 * A_WIDTH + 2 * B_HEADS * B_DK + 2 * B_WIDTH
    w_main = w_in[:, :n_main].astype(BF16)
    w_extra = _pad_cols(w_in[:, n_main:], LANES).astype(BF16)
    p_main, p_extra = _norm_matmul(h, g, w_main, w_extra, F32)
    a_out = _sgu(p_main, v_ln_g, w_s, b_s.T)
    gate_w2_pad = jnp.pad(gate_w2, ((0, LANES - B_RANK), (0, 0))).astype(BF16)
    b_out = _gla(p_main, p_extra, gate_w2_pad, gate_b, out_norm_g, bsz, seq)
    w_out = w_out.astype(BF16)
    return _proj_residual(h, [a_out, b_out], [w_out[:A_WIDTH], w_out[A_WIDTH:]])


def _odd_mixer(h, g, w_in, w_out, bias_tiles, rel_bias, bsz, seq):
    n_main = C_WIDTH + 2 * C_KVW + C_IDX_HEADS * C_IDX_DIM
    q_scale = (C_HEAD_DIM ** -0.5) * math.log2(math.e)
    col_scale = jnp.where(jnp.arange(n_main) < C_WIDTH, q_scale, 1.0).astype(F32)
    w_main = (w_in[:, :n_main] * col_scale).astype(BF16)
    w_extra = _pad_cols(w_in[:, n_main:], LANES).astype(BF16)
    p_main, p_extra = _norm_matmul(h, g, w_main, w_extra, BF16)
    o = _dsa(p_main, p_extra, bias_tiles, rel_bias, bsz, seq)
    return _proj_residual(h, [o], [w_out.astype(BF16)])


def kernel(x, norm_mix_g, norm_ffn_g, final_norm_g, ab_w_in, a_v_ln_g, a_w_s, a_b_s, b_gate_w2,
           b_gate_b, b_out_norm_g, ab_w_out, c_w_in, c_w_out, rel_bias, ffn_w1, ffn_w2):
    bsz, seq, d = x.shape
    depth = norm_mix_g.shape[0]
    h = x.reshape(bsz * seq, d)
    bias_tiles = _bias_tiles(rel_bias)
    for layer in range(depth):
        i = layer // 2
        if layer % 2 == 0:
            h = _even_mixer(h, norm_mix_g[layer], ab_w_in[i], a_v_ln_g[i], a_w_s[i], a_b_s[i],
                            b_gate_w2[i], b_gate_b[i], b_out_norm_g[i], ab_w_out[i], bsz, seq)
        else:
            h = _odd_mixer(h, norm_mix_g[layer], c_w_in[i], c_w_out[i], bias_tiles, rel_bias, bsz, seq)
        h = _ffn(h, norm_ffn_g[layer], ffn_w1[layer].astype(BF16), ffn_w2[layer].astype(BF16),
                 final_norm_g, layer == depth - 1)
    return h.reshape(bsz, seq, d)
```

```python
import functools
import math

import jax
import jax.numpy as jnp
from jax import lax
from jax.experimental import pallas as pl
from jax.experimental.pallas import tpu as pltpu

F32 = jnp.float32
BF16 = jnp.bfloat16
EPS = 1e-6

VMEM_LIMIT_BYTES = 56 * 1024 * 1024
LANES = 128
SUBLANES = 8

A_GROUPS = 8
A_DIM = 128
A_CHUNK = 128
A_WIDTH = A_GROUPS * A_DIM
B_HEADS = 4
B_DK = 128
B_DV = 256
B_RANK = 16
B_TAU = 16.0
B_CHUNK = 64
B_WIDTH = B_HEADS * B_DV
C_HEADS = 16
C_KV_HEADS = 4
C_GROUP = C_HEADS // C_KV_HEADS
C_HEAD_DIM = 128
C_IDX_HEADS = 16
C_IDX_DIM = 64
C_TOPK_MAX = 256
C_QBLOCK = 128
C_WIDTH = C_HEADS * C_HEAD_DIM
C_KVW = C_KV_HEADS * C_HEAD_DIM
C_SEL_ROWS = 4 * C_QBLOCK
C_IDX_ROWS = 4 * C_QBLOCK
C_ATT_ROWS = 4 * C_QBLOCK
C_VPAD = 16
C_BISECT_WARMUP = 14
C_DIRECT_EXP_LIMIT = 60.0
REL_BUCKETS = 32
REL_MAX_DIST = 128
MASK_NEG = -1e30


def _params(*sem):
    return pltpu.CompilerParams(dimension_semantics=sem, vmem_limit_bytes=VMEM_LIMIT_BYTES)


def _rms_rows(x, g):
    ms = jnp.mean(x * x, axis=-1, keepdims=True)
    return x * lax.rsqrt(ms + EPS) * g


def _norm_matmul_kernel(x_ref, g_ref, w_ref, we_ref, o_ref, oe_ref, xn_ref):
    @pl.when(pl.program_id(1) == 0)
    def _():
        xn = _rms_rows(x_ref[...], g_ref[...]).astype(BF16)
        xn_ref[...] = xn
        oe_ref[...] = jnp.dot(xn, we_ref[...], preferred_element_type=F32)

    o_ref[...] = jnp.dot(xn_ref[...], w_ref[...], preferred_element_type=F32).astype(o_ref.dtype)


def _norm_matmul(x, g, w_all, nm, w_extra, out_dtype, tm=1024, tn=1024):
    n, d = x.shape
    assert nm % tn == 0 and nm <= w_all.shape[1]
    return pl.pallas_call(
        _norm_matmul_kernel,
        grid=(n // tm, nm // tn),
        in_specs=[
            pl.BlockSpec((tm, d), lambda i, j: (i, 0)),
            pl.BlockSpec((1, d), lambda i, j: (0, 0)),
            pl.BlockSpec((d, tn), lambda i, j: (0, j)),
            pl.BlockSpec((d, LANES), lambda i, j: (0, 0)),
        ],
        out_specs=[
            pl.BlockSpec((tm, tn), lambda i, j: (i, j)),
            pl.BlockSpec((tm, LANES), lambda i, j: (i, 0)),
        ],
        out_shape=[jax.ShapeDtypeStruct((n, nm), out_dtype), jax.ShapeDtypeStruct((n, LANES), F32)],
        scratch_shapes=[pltpu.VMEM((tm, d), BF16)],
        compiler_params=_params("parallel", "arbitrary"),
        name="norm_matmul",
    )(x, g.reshape(1, d), w_all, w_extra)


def _proj_res_kernel(*refs, n_in):
    h_ref = refs[0]
    o_ref = refs[1 + 2 * n_in]
    acc = h_ref[...]
    for k in range(n_in):
        acc = acc + jnp.dot(refs[1 + k][...], refs[1 + n_in + k][...], preferred_element_type=F32)
    o_ref[...] = acc


def _proj_residual(h, xs, w, tm=512, tn=2048):
    n, d = h.shape
    n_in = len(xs)
    kx = xs[0].shape[1]
    assert all(x.shape[1] == kx for x in xs) and kx * n_in == w.shape[0]
    ws = [w] * n_in
    in_specs = [pl.BlockSpec((tm, tn), lambda i, j: (i, j))]
    in_specs += [pl.BlockSpec((tm, kx), lambda i, j: (i, 0)) for _ in xs]
    in_specs += [pl.BlockSpec((kx, tn), functools.partial(lambda i, j, k: (k, j), k=k)) for k in range(n_in)]
    return pl.pallas_call(
        functools.partial(_proj_res_kernel, n_in=n_in),
        grid=(n // tm, d // tn),
        in_specs=in_specs,
        out_specs=pl.BlockSpec((tm, tn), lambda i, j: (i, j)),
        out_shape=jax.ShapeDtypeStruct((n, d), F32),
        compiler_params=_params("parallel", "parallel"),
        name="proj_residual",
    )(h, *xs, *ws)


def _ffn_kernel(x_ref, g_ref, w1_ref, w2_ref, fg_ref, o_ref, xn_ref, acc_ref, *, final_norm):
    f = pl.program_id(1)

    @pl.when(f == 0)
    def _():
        xn_ref[...] = _rms_rows(x_ref[...], g_ref[...]).astype(BF16)
        acc_ref[...] = jnp.zeros_like(acc_ref)

    h1 = jnp.dot(xn_ref[...], w1_ref[...], preferred_element_type=F32)
    h1 = jnp.square(jnp.maximum(h1, 0.0)).astype(BF16)
    acc_ref[...] += jnp.dot(h1, w2_ref[...], preferred_element_type=F32)

    @pl.when(f == pl.num_programs(1) - 1)
    def _():
        y = x_ref[...] + acc_ref[...]
        if final_norm:
            y = _rms_rows(y, fg_ref[...])
        o_ref[...] = y


def _ffn(x, g, w1, w2, final_g, final_norm, tm=512, tf=1024):
    n, d = x.shape
    dff = w1.shape[1]
    return pl.pallas_call(
        functools.partial(_ffn_kernel, final_norm=final_norm),
        grid=(n // tm, dff // tf),
        in_specs=[
            pl.BlockSpec((tm, d), lambda i, f: (i, 0)),
            pl.BlockSpec((1, d), lambda i, f: (0, 0)),
            pl.BlockSpec((d, tf), lambda i, f: (0, f)),
            pl.BlockSpec((tf, d), lambda i, f: (f, 0)),
            pl.BlockSpec((1, d), lambda i, f: (0, 0)),
        ],
        out_specs=pl.BlockSpec((tm, d), lambda i, f: (i, 0)),
        out_shape=jax.ShapeDtypeStruct((n, d), F32),
        scratch_shapes=[pltpu.VMEM((tm, d), BF16), pltpu.VMEM((tm, d), F32)],
        compiler_params=_params("parallel", "arbitrary"),
        name="ffn",
    )(x, g.reshape(1, d), w1, w2, final_g.reshape(1, d))


def _gelu(x):
    return 0.5 * x * (1.0 + lax.erf(x * math.sqrt(0.5)))


def _sgu_kernel(u_ref, v_ref, lng_ref, w_ref, bt_ref, o_ref):
    t_i = lax.broadcasted_iota(jnp.int32, (A_CHUNK, A_CHUNK), 0)
    s_i = lax.broadcasted_iota(jnp.int32, (A_CHUNK, A_CHUNK), 1)
    causal = s_i <= t_i
    n_sub = u_ref.shape[0] // A_CHUNK
    for g in range(A_GROUPS):
        sl = slice(g * A_DIM, (g + 1) * A_DIM)
        w = jnp.where(causal, w_ref[g], 0.0).astype(BF16)
        for sub in range(n_sub):
            rsl = slice(sub * A_CHUNK, (sub + 1) * A_CHUNK)
            v = _gelu(v_ref[rsl, sl])
            mu = jnp.mean(v, axis=-1, keepdims=True)
            vc = v - mu
            var = jnp.mean(vc * vc, axis=-1, keepdims=True)
            vn = vc * lax.rsqrt(var + EPS) * lng_ref[g:g + 1, :]
            z = jnp.dot(w, vn.astype(BF16), preferred_element_type=F32) + bt_ref[:, g:g + 1]
            o_ref[rsl, sl] = (_gelu(u_ref[rsl, sl]) * z).astype(o_ref.dtype)


def _sgu(p_main, ln_g, w_s, b_s_t, chunks_per_step=4):
    n = p_main.shape[0]
    rows = A_CHUNK * chunks_per_step
    return pl.pallas_call(
        _sgu_kernel,
        grid=(n // rows,),
        in_specs=[
            pl.BlockSpec((rows, A_WIDTH), lambda i: (i, 0)),
            pl.BlockSpec((rows, A_WIDTH), lambda i: (i, 1)),
            pl.BlockSpec((A_GROUPS, A_DIM), lambda i: (0, 0)),
            pl.BlockSpec((A_GROUPS, A_CHUNK, A_CHUNK), lambda i: (0, 0, 0)),
            pl.BlockSpec((A_CHUNK, A_GROUPS), lambda i: (0, 0)),
        ],
        out_specs=pl.BlockSpec((rows, A_WIDTH), lambda i: (i, 0)),
        out_shape=jax.ShapeDtypeStruct((n, A_WIDTH), BF16),
        compiler_params=_params("parallel"),
        name="sgu",
    )(p_main, p_main, ln_g, w_s, b_s_t)


def _cumsum_rows(x):
    n = x.shape[0]
    row = lax.broadcasted_iota(jnp.int32, x.shape, 0)
    sh = 1
    while sh < n:
        x = x + jnp.where(row >= sh, pltpu.roll(x, sh, axis=0), 0.0)
        sh *= 2
    return x


def _log_sigmoid(x):
    return jnp.minimum(x, 0.0) - jnp.log1p(jnp.exp(-jnp.abs(x)))


def _gla_kernel(q_ref, k_ref, v_ref, r_ref, g_ref, w2_ref, gb_ref, ng_ref, o_ref, s_ref):
    @pl.when(pl.program_id(1) == 0)
    def _():
        s_ref[...] = jnp.zeros_like(s_ref)

    c = B_CHUNK
    n_sub = q_ref.shape[0] // c
    gate = jnp.dot(g_ref[...].astype(BF16), w2_ref[...], preferred_element_type=F32) + gb_ref[...]
    log_a = _log_sigmoid(gate) * (1.0 / B_TAU)
    i_i = lax.broadcasted_iota(jnp.int32, (c, c), 0)
    j_i = lax.broadcasted_iota(jnp.int32, (c, c), 1)
    causal = j_i <= i_i
    for h in range(B_HEADS):
        ksl = slice(h * B_DK, (h + 1) * B_DK)
        vsl = slice(h * B_DV, (h + 1) * B_DV)
        state = s_ref[h]
        for sub in range(n_sub):
            rsl = slice(sub * c, (sub + 1) * c)
            cum = _cumsum_rows(log_a[rsl, ksl])
            last = cum[c - 1:c, :]
            ref = 0.5 * last
            q = q_ref[rsl, ksl] * (B_DK ** -0.5)
            k = k_ref[rsl, ksl]
            v = v_ref[rsl, vsl].astype(BF16)
            qe = (q * jnp.exp(cum - ref)).astype(BF16)
            ke = (k * jnp.exp(ref - cum)).astype(BF16)
            scores = lax.dot_general(qe, ke, (((1,), (1,)), ((), ())), preferred_element_type=F32)
            scores = jnp.where(causal, scores, 0.0).astype(BF16)
            o = jnp.dot(scores, v, preferred_element_type=F32)
            q_inter = (q * jnp.exp(cum)).astype(BF16)
            o = o + jnp.dot(q_inter, state.astype(BF16), preferred_element_type=F32)
            cum_t = cum.T
            last_t = cum_t[:, c - 1:c]
            k_state_t = (k.T * jnp.exp(last_t - cum_t)).astype(BF16)
            state = jnp.exp(last_t) * state + jnp.dot(k_state_t, v, preferred_element_type=F32)
            o = _rms_rows(o, ng_ref[...])
            r = r_ref[rsl, vsl]
            o_ref[rsl, vsl] = (o * (r * jax.nn.sigmoid(r))).astype(o_ref.dtype)
        s_ref[h] = state


def _gla(p_main, p_extra, gate_w2_pad, gate_b, norm_g, bsz, seq, chunks_per_step=4):
    n = p_main.shape[0]
    c = B_CHUNK * chunks_per_step
    nc = seq // c
    qk_w = B_HEADS * B_DK
    row = lambda b, i: b * nc + i
    return pl.pallas_call(
        _gla_kernel,
        grid=(bsz, nc),
        in_specs=[
            pl.BlockSpec((c, qk_w), lambda b, i: (row(b, i), (2 * A_WIDTH) // qk_w)),
            pl.BlockSpec((c, qk_w), lambda b, i: (row(b, i), (2 * A_WIDTH) // qk_w + 1)),
            pl.BlockSpec((c, B_WIDTH), lambda b, i: (row(b, i), (2 * A_WIDTH + 2 * qk_w) // B_WIDTH)),
            pl.BlockSpec((c, B_WIDTH), lambda b, i: (row(b, i), (2 * A_WIDTH + 2 * qk_w) // B_WIDTH + 1)),
            pl.BlockSpec((c, LANES), lambda b, i: (row(b, i), 0)),
            pl.BlockSpec((LANES, qk_w), lambda b, i: (0, 0)),
            pl.BlockSpec((1, qk_w), lambda b, i: (0, 0)),
            pl.BlockSpec((1, B_DV), lambda b, i: (0, 0)),
        ],
        out_specs=pl.BlockSpec((c, B_WIDTH), lambda b, i: (row(b, i), 0)),
        out_shape=jax.ShapeDtypeStruct((n, B_WIDTH), BF16),
        scratch_shapes=[pltpu.VMEM((B_HEADS, B_DK, B_DV), F32)],
        compiler_params=_params("parallel", "arbitrary"),
        name="gla",
    )(p_main, p_main, p_main, p_main, p_extra, gate_w2_pad, gate_b.reshape(1, qk_w),
      norm_g.reshape(1, B_DV))


def _t5_bucket(dist):
    max_exact = REL_BUCKETS // 2
    d = jnp.maximum(dist, 1).astype(F32)
    large = max_exact + (jnp.log(d / max_exact) / math.log(REL_MAX_DIST / max_exact)
                         * (REL_BUCKETS - max_exact)).astype(jnp.int32)
    large = jnp.minimum(large, REL_BUCKETS - 1)
    return jnp.where(dist < max_exact, dist, large)


def _bias_kernel(rb_ref, o_ref):
    s_i = lax.broadcasted_iota(jnp.int32, (C_QBLOCK, C_QBLOCK), 0)
    t_i = lax.broadcasted_iota(jnp.int32, (C_QBLOCK, C_QBLOCK), 1)
    for delta in range(2):
        bucket = _t5_bucket(jnp.maximum(delta * C_QBLOCK + t_i - s_i, 0))
        for h in range(C_HEADS):
            acc = jnp.zeros((C_QBLOCK, C_QBLOCK), F32)
            for b in range(REL_BUCKETS):
                acc = jnp.where(bucket == b, rb_ref[b, h], acc)
            o_ref[h, delta] = acc


def _bias_tiles(rel_bias):
    return pl.pallas_call(
        _bias_kernel,
        in_specs=[pl.BlockSpec(memory_space=pltpu.SMEM)],
        out_specs=pl.BlockSpec(memory_space=pltpu.VMEM),
        out_shape=jax.ShapeDtypeStruct((C_HEADS, 2, C_QBLOCK, C_QBLOCK), F32),
        name="bias_tiles",
    )(rel_bias)


def _dsa_kernel(rb_ref, q_ref, kv_ref, iq_ref, ex_ref, bias_ref, hsel_ref, o_ref,
                keys_ref, mask_ref, vt_ref, vtw_ref, iklo_ref, ikhi_ref, iqa_ref, qg_ref,
                acc_ref, m_ref, bvec_ref, *, seq, ksel):
    qb = pl.program_id(1)
    blk = C_QBLOCK
    nb = seq // blk
    n_pair = C_IDX_HEADS // 2
    n_sel = lax.shift_right_logical(qb, 2) + 1
    sel_tiles = C_SEL_ROWS // SUBLANES
    lane = lax.broadcasted_iota(jnp.int32, (blk, LANES), 1)
    t_row = lax.broadcasted_iota(jnp.int32, (1, blk), 1)
    d_aug = C_HEAD_DIM + C_VPAD

    def ones_row(width):
        r = lax.broadcasted_iota(jnp.int32, (C_VPAD, width), 0)
        return jnp.where(r == 0, 1.0, 0.0).astype(BF16)

    @pl.when(qb == 0)
    def _():
        def fill(j, k_sq):
            rows = pl.ds(pl.multiple_of(j * blk, blk), blk)
            ex = ex_ref[rows, :]
            iklo_ref[rows, :] = jnp.where(lane < C_IDX_DIM, ex, 0.0).astype(BF16)
            ikhi_ref[rows, :] = jnp.where(lane >= C_IDX_DIM, pltpu.roll(ex, C_IDX_DIM, axis=1),
                                          0.0).astype(BF16)
            for kvh in range(C_KV_HEADS):
                vblk = kv_ref[rows, C_KVW + kvh * C_HEAD_DIM:C_KVW + (kvh + 1) * C_HEAD_DIM]
                vt_ref[kvh, j, 0:C_HEAD_DIM, :] = vblk.astype(F32).T.astype(BF16)
                vt_ref[kvh, j, C_HEAD_DIM:d_aug, :] = ones_row(blk)
            k_sq_new = []
            for kvh in range(C_KV_HEADS):
                kblk = kv_ref[rows, kvh * C_HEAD_DIM:(kvh + 1) * C_HEAD_DIM].astype(F32)
                k_sq_new.append(jnp.maximum(k_sq[kvh], jnp.sum(kblk * kblk, axis=1, keepdims=True)))
            return tuple(k_sq_new)
        k_sq = lax.fori_loop(0, nb, fill, tuple(jnp.zeros((blk, 1), F32) for _ in range(C_KV_HEADS)))
        lane_row = lax.broadcasted_iota(jnp.int32, (1, LANES), 1)
        kn_row = jnp.zeros((1, LANES), F32)
        b_row = jnp.zeros((1, LANES), F32)
        for kvh in range(C_KV_HEADS):
            kn_row = jnp.where(lax.div(lane_row, C_GROUP) == kvh, jnp.max(jnp.sqrt(k_sq[kvh])), kn_row)
        for h in range(C_HEADS):
            b_abs = jnp.abs(rb_ref[0, h])
            for b in range(1, REL_BUCKETS):
                b_abs = jnp.maximum(b_abs, jnp.abs(rb_ref[b, h]))
            b_row = jnp.where(lane_row == h, b_abs * math.log2(math.e), b_row)
        bvec_ref[0:1, :] = kn_row
        bvec_ref[1:2, :] = b_row

        def fill_wide(i, carry):
            rows = pl.ds(pl.multiple_of(i * C_ATT_ROWS, C_ATT_ROWS), C_ATT_ROWS)
            for kvh in range(C_KV_HEADS):
                vblk = kv_ref[rows, C_KVW + kvh * C_HEAD_DIM:C_KVW + (kvh + 1) * C_HEAD_DIM]
                vtw_ref[kvh, i, 0:C_HEAD_DIM, :] = vblk.astype(F32).T.astype(BF16)
                vtw_ref[kvh, i, C_HEAD_DIM:d_aug, :] = ones_row(C_ATT_ROWS)
            return carry
        lax.fori_loop(0, seq // C_ATT_ROWS, fill_wide, 0)

    q_rows = pl.ds(pl.multiple_of(qb * blk, blk), blk)
    ext_t = ex_ref[q_rows, :].T
    idx_scale = (C_IDX_DIM ** -0.5) * (C_IDX_HEADS ** -0.5)
    w_idx = [ext_t[C_IDX_DIM + h:C_IDX_DIM + h + 1, :] * idx_scale for h in range(C_IDX_HEADS)]
    for p in range(n_pair):
        iqa_ref[p * blk:(p + 1) * blk, :] = iq_ref[:, p * LANES:(p + 1) * LANES]
    for kvh in range(C_KV_HEADS):
        for g in range(C_GROUP):
            h = kvh * C_GROUP + g
            qg_ref[kvh, g * blk:(g + 1) * blk, :] = q_ref[:, h * C_HEAD_DIM:(h + 1) * C_HEAD_DIM]

    s_iu = lax.broadcasted_iota(jnp.int32, (C_IDX_ROWS, blk), 0)
    t_iu = lax.broadcasted_iota(jnp.int32, (C_IDX_ROWS, blk), 1)
    nt = (((1,), (1,)), ((), ()))

    def idx_body(u, carry):
        mx, mn = carry
        rows = pl.ds(pl.multiple_of(u * C_IDX_ROWS, C_IDX_ROWS), C_IDX_ROWS)
        ik_lo = iklo_ref[rows, :]
        ik_hi = ikhi_ref[rows, :]
        acc = jnp.zeros((C_IDX_ROWS, blk), F32)
        for c in range(n_pair // 2):
            w = iqa_ref[c * 2 * blk:(c + 1) * 2 * blk, :]
            sc_even = lax.dot_general(ik_lo, w, nt, preferred_element_type=F32)
            sc_odd = lax.dot_general(ik_hi, w, nt, preferred_element_type=F32)
            for pp in range(2):
                p = 2 * c + pp
                sl = slice(pp * blk, (pp + 1) * blk)
                acc = acc + jnp.maximum(sc_even[:, sl], 0.0) * w_idx[2 * p]
                acc = acc + jnp.maximum(sc_odd[:, sl], 0.0) * w_idx[2 * p + 1]
        causal = (u * C_IDX_ROWS + s_iu) <= (qb * blk + t_iu)
        keys_ref[rows, :] = jnp.where(causal, acc, -jnp.inf)
        mx = jnp.maximum(mx, jnp.max(jnp.where(causal, acc, -jnp.inf), axis=0, keepdims=True))
        mn = jnp.minimum(mn, jnp.min(jnp.where(causal, acc, jnp.inf), axis=0, keepdims=True))
        return mx, mn

    mx, mn = lax.fori_loop(0, n_sel * (C_SEL_ROWS // C_IDX_ROWS), idx_body,
                           (jnp.full((1, blk), -jnp.inf, F32), jnp.full((1, blk), jnp.inf, F32)))

    def sel_rows(i):
        return pl.ds(pl.multiple_of(i * C_SEL_ROWS, C_SEL_ROWS), C_SEL_ROWS)

    def tree_sum(x):
        parts = [x[i] for i in range(x.shape[0])]
        while len(parts) > 1:
            parts = [parts[i] + parts[i + 1] for i in range(0, len(parts), 2)]
        return parts[0]

    n_lanes_acc = 8

    def count(pred):
        def body(i, cs):
            kb = keys_ref[sel_rows(i), :].reshape(sel_tiles, SUBLANES, blk)
            hit = pred(kb, i)
            cs = list(cs)
            for tile in range(sel_tiles):
                a = tile % n_lanes_acc
                cs[a] = jnp.where(hit[tile], cs[a] + 1.0, cs[a])
            return tuple(cs)
        cs = lax.fori_loop(0, n_sel, body, tuple(jnp.zeros((SUBLANES, blk), F32) for _ in range(n_lanes_acc)))
        return jnp.sum(tree_sum(jnp.stack(cs)), axis=0, keepdims=True)

    n_causal = (qb * blk + t_row + 1).astype(F32)
    k_eff = jnp.minimum(n_causal, float(ksel))
    lo0 = mn
    hi0 = mx + (jnp.abs(mx) * 1e-6 + 1e-30)
    cnt0 = n_causal

    def active_of(lo, hi, cnt):
        mid = lo + 0.5 * (hi - lo)
        collapsed = (mid <= lo) | (mid >= hi)
        return mid, (cnt > k_eff) & jnp.logical_not(collapsed)

    def bis_cond(st):
        return st[3] > 0.0

    def bis_step(_, st):
        lo, hi, cnt = st
        mid, active = active_of(lo, hi, cnt)
        c = count(lambda kb, i: kb >= mid)
        ge = c >= k_eff
        return (jnp.where(active & ge, mid, lo), jnp.where(active & jnp.logical_not(ge), mid, hi),
                jnp.where(active & ge, c, cnt))

    def n_active(st):
        _, active = active_of(*st)
        return jnp.sum(jnp.where(active, 1.0, 0.0))

    def bis_body(st):
        st = lax.fori_loop(0, 2, bis_step, st[:3])
        return (*st, n_active(st))

    n_warm = jnp.where(qb * blk >= ksel, C_BISECT_WARMUP, 0)
    st = lax.fori_loop(0, n_warm, bis_step, (lo0, hi0, cnt0))
    thr, _, cnt, _ = lax.while_loop(bis_cond, bis_body, (*st, n_active(st)))

    def mask_body(i, carry):
        mask_ref[sel_rows(i), :] = jnp.where(keys_ref[sel_rows(i), :] >= thr, 0.0, MASK_NEG)
        return carry
    lax.fori_loop(0, n_sel, mask_body, 0)

    tie = cnt > k_eff

    @pl.when(jnp.sum(jnp.where(tie, 1.0, 0.0)) > 0.0)
    def _():
        tile_i = lax.broadcasted_iota(jnp.int32, (sel_tiles, SUBLANES, blk), 0)
        sub_i = lax.broadcasted_iota(jnp.int32, (sel_tiles, SUBLANES, blk), 1)
        s_i2 = lax.broadcasted_iota(jnp.int32, (C_SEL_ROWS, blk), 0)

        def pos3(i):
            return (i * C_SEL_ROWS + tile_i * SUBLANES + sub_i).astype(F32)

        need = k_eff - count(lambda kb, i: kb > thr)
        lo_i = jnp.full((1, blk), -1.0, F32)
        hi_i = jnp.zeros((1, blk), F32) + (n_sel * C_SEL_ROWS - 1).astype(F32)
        n_steps = int(math.ceil(math.log2(seq))) + 1

        def tb(_, st):
            lo_i, hi_i = st
            mid = jnp.floor((lo_i + hi_i) * 0.5)
            c = count(lambda kb, i: (kb == thr) & (pos3(i) <= mid))
            ge = c >= need
            return jnp.where(ge, lo_i, mid), jnp.where(ge, mid, hi_i)
        _, cut = lax.fori_loop(0, n_steps, tb, (lo_i, hi_i))

        def fix_body(i, carry):
            kb = keys_ref[sel_rows(i), :]
            pos = (i * C_SEL_ROWS + s_i2).astype(F32)
            sel = (kb > thr) | ((kb == thr) & (pos <= cut))
            mask_ref[sel_rows(i), :] = jnp.where(tie, jnp.where(sel, 0.0, MASK_NEG), mask_ref[sel_rows(i), :])
            return carry
        lax.fori_loop(0, n_sel, fix_body, 0)

    log2e = math.log2(math.e)
    n_far = jnp.maximum(qb - 1, 0)
    n_far_wide = lax.div(n_far, C_ATT_ROWS // blk)

    def qk_logits(rows):
        return [lax.dot_general(kv_ref[rows, kvh * C_HEAD_DIM:(kvh + 1) * C_HEAD_DIM], qg_ref[kvh], nt,
                                preferred_element_type=F32) for kvh in range(C_KV_HEADS)]

    def far_bias(h):
        return rb_ref[REL_BUCKETS - 1, h] * log2e

    def attend_online(row0, n_rows, delta, v_tile):
        rows = pl.ds(pl.multiple_of(row0, n_rows), n_rows)
        madd = mask_ref[rows, :]
        logits = qk_logits(rows)
        results = []
        for kvh in range(C_KV_HEADS):
            lg = logits[kvh]
            m_old = m_ref[kvh]
            ys, m_parts, shifts = [], [], []
            for g in range(C_GROUP):
                h = kvh * C_GROUP + g
                sl = slice(g * blk, (g + 1) * blk)
                if delta is None:
                    y = lg[:, sl] + madd
                    m_g = jnp.maximum(m_old[:, sl], jnp.max(y, axis=0, keepdims=True) + far_bias(h))
                    shifts.append(m_g - far_bias(h))
                else:
                    y = lg[:, sl] + (madd + bias_ref[h, delta] * log2e)
                    m_g = jnp.maximum(m_old[:, sl], jnp.max(y, axis=0, keepdims=True))
                    shifts.append(m_g)
                ys.append(y)
                m_parts.append(m_g)
            m_new = jnp.concatenate(m_parts, axis=1)
            alpha = jnp.exp2(m_old - m_new)
            p = jnp.concatenate([jnp.exp2(ys[g] - shifts[g]) for g in range(C_GROUP)], axis=1)
            pv = jnp.dot(v_tile(kvh), p.astype(BF16), preferred_element_type=F32)
            results.append((m_new, acc_ref[kvh] * alpha + pv))
        for kvh, (m_new, acc_new) in enumerate(results):
            m_ref[kvh] = m_new
            acc_ref[kvh] = acc_new

    def attend_direct(row0, n_rows, delta, v_tile):
        rows = pl.ds(pl.multiple_of(row0, n_rows), n_rows)
        madd = mask_ref[rows, :]
        logits = qk_logits(rows)
        results = []
        for kvh in range(C_KV_HEADS):
            lg = logits[kvh]
            parts = []
            for g in range(C_GROUP):
                h = kvh * C_GROUP + g
                sl = slice(g * blk, (g + 1) * blk)
                if delta is None:
                    parts.append(jnp.exp2(lg[:, sl] + madd))
                else:
                    parts.append(jnp.exp2(lg[:, sl] + (madd + bias_ref[h, delta] * log2e)))
            p = jnp.concatenate(parts, axis=1).astype(BF16)
            results.append(acc_ref[kvh] + jnp.dot(v_tile(kvh), p, preferred_element_type=F32))
        for kvh, acc_new in enumerate(results):
            acc_ref[kvh] = acc_new

    def apply_far_bias():
        for kvh in range(C_KV_HEADS):
            row = jnp.concatenate([jnp.exp2(jnp.full((1, blk), far_bias(kvh * C_GROUP + g), F32))
                                   for g in range(C_GROUP)], axis=1)
            acc_ref[kvh] = acc_ref[kvh] * row

    def run_attention(attend, after_far):
        acc_ref[...] = jnp.zeros_like(acc_ref)

        def attend_block(j, delta):
            attend(j * blk, blk, delta, lambda kvh: vt_ref[kvh, j])

        def far_wide_body(i, carry):
            attend(i * C_ATT_ROWS, C_ATT_ROWS, None, lambda kvh: vtw_ref[kvh, i])
            return carry
        lax.fori_loop(0, n_far_wide, far_wide_body, 0)

        def far_body(j, carry):
            attend_block(j, None)
            return carry
        lax.fori_loop(n_far_wide * (C_ATT_ROWS // blk), n_far, far_body, 0)
        after_far()

        @pl.when(qb >= 1)
        def _():
            attend_block(qb - 1, 1)

        attend_block(qb, 0)

    q_all = q_ref[...].astype(F32)
    q_sq = jnp.dot((q_all * q_all).astype(BF16), hsel_ref[...], preferred_element_type=F32)
    bound = jnp.sqrt(q_sq) * bvec_ref[0:1, :] + bvec_ref[1:2, :]
    direct_ok = jnp.max(bound) * 1.05 < C_DIRECT_EXP_LIMIT

    @pl.when(direct_ok)
    def _():
        run_attention(attend_direct, apply_far_bias)

    @pl.when(jnp.logical_not(direct_ok))
    def _():
        m_ref[...] = jnp.full(m_ref.shape, MASK_NEG, F32)
        run_attention(attend_online, lambda: None)

    for kvh in range(C_KV_HEADS):
        o_t = acc_ref[kvh, 0:C_HEAD_DIM, :] / acc_ref[kvh, C_HEAD_DIM:C_HEAD_DIM + 1, :]
        for g in range(C_GROUP):
            h = kvh * C_GROUP + g
            o_ref[:, h * C_HEAD_DIM:(h + 1) * C_HEAD_DIM] = o_t[:, g * blk:(g + 1) * blk].T.astype(o_ref.dtype)


def _dsa(p_main, p_extra, bias_tiles, rel_bias, bsz, seq):
    assert seq % C_SEL_ROWS == 0 and seq % C_ATT_ROWS == 0
    n = p_main.shape[0]
    nb = seq // C_QBLOCK
    blk = C_QBLOCK
    ksel = min(C_TOPK_MAX, seq // 4)
    n_pair = C_IDX_HEADS // 2
    d_aug = C_HEAD_DIM + C_VPAD
    head_sel = jnp.repeat(jnp.eye(C_HEADS, LANES, dtype=BF16), C_HEAD_DIM, axis=0)
    row = lambda b, i: b * nb + i
    return pl.pallas_call(
        functools.partial(_dsa_kernel, seq=seq, ksel=ksel),
        grid=(bsz, nb),
        in_specs=[
            pl.BlockSpec(memory_space=pltpu.SMEM),
            pl.BlockSpec((blk, C_WIDTH), lambda b, i: (row(b, i), 0)),
            pl.BlockSpec((seq, 2 * C_KVW), lambda b, i: (b, C_WIDTH // (2 * C_KVW))),
            pl.BlockSpec((blk, C_IDX_HEADS * C_IDX_DIM),
                         lambda b, i: (row(b, i), (C_WIDTH + 2 * C_KVW) // (C_IDX_HEADS * C_IDX_DIM))),
            pl.BlockSpec((seq, LANES), lambda b, i: (b, 0)),
            pl.BlockSpec((C_HEADS, 2, blk, blk), lambda b, i: (0, 0, 0, 0)),
            pl.BlockSpec((C_WIDTH, LANES), lambda b, i: (0, 0)),
        ],
        out_specs=pl.BlockSpec((blk, C_WIDTH), lambda b, i: (row(b, i), 0)),
        out_shape=jax.ShapeDtypeStruct((n, C_WIDTH), BF16),
        scratch_shapes=[
            pltpu.VMEM((seq, blk), F32),
            pltpu.VMEM((seq, blk), F32),
            pltpu.VMEM((C_KV_HEADS, nb, d_aug, blk), BF16),
            pltpu.VMEM((C_KV_HEADS, seq // C_ATT_ROWS, d_aug, C_ATT_ROWS), BF16),
            pltpu.VMEM((seq, LANES), BF16),
            pltpu.VMEM((seq, LANES), BF16),
            pltpu.VMEM((n_pair * blk, LANES), BF16),
            pltpu.VMEM((C_KV_HEADS, C_GROUP * blk, C_HEAD_DIM), BF16),
            pltpu.VMEM((C_KV_HEADS, d_aug, C_GROUP * blk), F32),
            pltpu.VMEM((C_KV_HEADS, 1, C_GROUP * blk), F32),
            pltpu.VMEM((SUBLANES, LANES), F32),
        ],
        compiler_params=_params("parallel", "arbitrary"),
        name="dsa",
    )(rel_bias, p_main, p_main, p_main, p_extra, bias_tiles, head_sel)


def _pad_cols(w, width):
    return jnp.pad(w, ((0, 0), (0, width - w.shape[1])))


def _even_mixer(h, g, w_in, v_ln_g, w_s, b_s, gate_w2, gate_b, out_norm_g, w_out, bsz, seq):
    n_main = 2 * A_WIDTH + 2 * B_HEADS * B_DK + 2 * B_WIDTH
    w_extra = _pad_cols(w_in[:, n_main:], LANES).astype(BF16)
    p_main, p_extra = _norm_matmul(h, g, w_in.astype(BF16), n_main, w_extra, F32)
    a_out = _sgu(p_main, v_ln_g, w_s, b_s.T)
    gate_w2_pad = jnp.pad(gate_w2, ((0, LANES - B_RANK), (0, 0))).astype(BF16)
    b_out = _gla(p_main, p_extra, gate_w2_pad, gate_b, out_norm_g, bsz, seq)
    return _proj_residual(h, [a_out, b_out], w_out.astype(BF16))


def _odd_mixer(h, g, w_in, w_out, bias_tiles, rel_bias, bsz, seq):
    n_main = C_WIDTH + 2 * C_KVW + C_IDX_HEADS * C_IDX_DIM
    q_scale = (C_HEAD_DIM ** -0.5) * math.log2(math.e)
    col_scale = jnp.where(jnp.arange(w_in.shape[1]) < C_WIDTH, q_scale, 1.0).astype(F32)
    w_extra = _pad_cols(w_in[:, n_main:], LANES).astype(BF16)
    p_main, p_extra = _norm_matmul(h, g, (w_in * col_scale).astype(BF16), n_main, w_extra, BF16)
    o = _dsa(p_main, p_extra, bias_tiles, rel_bias, bsz, seq)
    return _proj_residual(h, [o], w_out.astype(BF16))


def kernel(x, norm_mix_g, norm_ffn_g, final_norm_g, ab_w_in, a_v_ln_g, a_w_s, a_b_s, b_gate_w2,
           b_gate_b, b_out_norm_g, ab_w_out, c_w_in, c_w_out, rel_bias, ffn_w1, ffn_w2):
    bsz, seq, d = x.shape
    depth = norm_mix_g.shape[0]
    h = x.reshape(bsz * seq, d)
    bias_tiles = _bias_tiles(rel_bias)
    for layer in range(depth):
        i = layer // 2
        if layer % 2 == 0:
            h = _even_mixer(h, norm_mix_g[layer], ab_w_in[i], a_v_ln_g[i], a_w_s[i], a_b_s[i],
                            b_gate_w2[i], b_gate_b[i], b_out_norm_g[i], ab_w_out[i], bsz, seq)
        else:
            h = _odd_mixer(h, norm_mix_g[layer], c_w_in[i], c_w_out[i], bias_tiles, rel_bias, bsz, seq)
        h = _ffn(h, norm_ffn_g[layer], ffn_w1[layer].astype(BF16), ffn_w2[layer].astype(BF16),
                 final_norm_g, layer == depth - 1)
    return h.reshape(bsz, seq, d)
```

```python
import functools
import math

import jax
import jax.numpy as jnp
from jax import lax
from jax.experimental import pallas as pl
from jax.experimental.pallas import tpu as pltpu

F32 = jnp.float32
BF16 = jnp.bfloat16
EPS = 1e-6

VMEM_LIMIT_BYTES = 56 * 1024 * 1024
LANES = 128
SUBLANES = 8

A_GROUPS = 8
A_DIM = 128
A_CHUNK = 128
A_WIDTH = A_GROUPS * A_DIM
B_HEADS = 4
B_DK = 128
B_DV = 256
B_RANK = 16
B_TAU = 16.0
B_CHUNK = 64
B_WIDTH = B_HEADS * B_DV
C_HEADS = 16
C_KV_HEADS = 4
C_GROUP = C_HEADS // C_KV_HEADS
C_HEAD_DIM = 128
C_IDX_HEADS = 16
C_IDX_DIM = 64
C_TOPK_MAX = 256
C_QBLOCK = 128
C_WIDTH = C_HEADS * C_HEAD_DIM
C_KVW = C_KV_HEADS * C_HEAD_DIM
C_SEL_ROWS = 4 * C_QBLOCK
C_IDX_ROWS = 4 * C_QBLOCK
C_ATT_ROWS = 4 * C_QBLOCK
C_VPAD = 16
C_BISECT_WARMUP = 14
C_DIRECT_EXP_LIMIT = 60.0
REL_BUCKETS = 32
REL_MAX_DIST = 128
MASK_NEG = -1e30


def _params(*sem):
    return pltpu.CompilerParams(dimension_semantics=sem, vmem_limit_bytes=VMEM_LIMIT_BYTES)


def _rms_rows(x, g):
    ms = jnp.mean(x * x, axis=-1, keepdims=True)
    return x * lax.rsqrt(ms + EPS) * g


def _norm_matmul_kernel(x_ref, g_ref, w_ref, we_ref, o_ref, oe_ref, xn_ref):
    @pl.when(pl.program_id(1) == 0)
    def _():
        xn = _rms_rows(x_ref[...], g_ref[...]).astype(BF16)
        xn_ref[...] = xn
        oe_ref[...] = jnp.dot(xn, we_ref[...], preferred_element_type=F32)

    o_ref[...] = jnp.dot(xn_ref[...], w_ref[...], preferred_element_type=F32).astype(o_ref.dtype)


def _norm_matmul(x, g, w_all, w_extra, li, nm, out_dtype, tm=1024, tn=1024):
    n, d = x.shape
    assert nm % tn == 0 and nm <= w_all.shape[2]
    return pl.pallas_call(
        _norm_matmul_kernel,
        grid=(n // tm, nm // tn),
        in_specs=[
            pl.BlockSpec((tm, d), lambda i, j: (i, 0)),
            pl.BlockSpec((1, d), lambda i, j: (0, 0)),
            pl.BlockSpec((None, d, tn), lambda i, j: (li, 0, j)),
            pl.BlockSpec((None, d, LANES), lambda i, j: (li, 0, 0)),
        ],
        out_specs=[
            pl.BlockSpec((tm, tn), lambda i, j: (i, j)),
            pl.BlockSpec((tm, LANES), lambda i, j: (i, 0)),
        ],
        out_shape=[jax.ShapeDtypeStruct((n, nm), out_dtype), jax.ShapeDtypeStruct((n, LANES), F32)],
        scratch_shapes=[pltpu.VMEM((tm, d), BF16)],
        compiler_params=_params("parallel", "arbitrary"),
        name="norm_matmul",
    )(x, g.reshape(1, d), w_all, w_extra)


def _proj_res_kernel(*refs, n_in):
    h_ref = refs[0]
    o_ref = refs[1 + 2 * n_in]
    acc = h_ref[...]
    for k in range(n_in):
        acc = acc + jnp.dot(refs[1 + k][...], refs[1 + n_in + k][...], preferred_element_type=F32)
    o_ref[...] = acc


def _proj_residual(h, xs, w, li, tm=512, tn=2048):
    n, d = h.shape
    n_in = len(xs)
    kx = xs[0].shape[1]
    assert all(x.shape[1] == kx for x in xs) and kx * n_in == w.shape[1]
    ws = [w] * n_in
    in_specs = [pl.BlockSpec((tm, tn), lambda i, j: (i, j))]
    in_specs += [pl.BlockSpec((tm, kx), lambda i, j: (i, 0)) for _ in xs]
    in_specs += [pl.BlockSpec((None, kx, tn), functools.partial(lambda i, j, k: (li, k, j), k=k))
                 for k in range(n_in)]
    return pl.pallas_call(
        functools.partial(_proj_res_kernel, n_in=n_in),
        grid=(n // tm, d // tn),
        in_specs=in_specs,
        out_specs=pl.BlockSpec((tm, tn), lambda i, j: (i, j)),
        out_shape=jax.ShapeDtypeStruct((n, d), F32),
        compiler_params=_params("parallel", "parallel"),
        name="proj_residual",
    )(h, *xs, *ws)


def _ffn_kernel(x_ref, g_ref, w1_ref, w2_ref, fg_ref, o_ref, xn_ref, acc_ref, *, final_norm):
    f = pl.program_id(1)

    @pl.when(f == 0)
    def _():
        xn_ref[...] = _rms_rows(x_ref[...], g_ref[...]).astype(BF16)
        acc_ref[...] = jnp.zeros_like(acc_ref)

    h1 = jnp.dot(xn_ref[...], w1_ref[...], preferred_element_type=F32)
    h1 = jnp.square(jnp.maximum(h1, 0.0)).astype(BF16)
    acc_ref[...] += jnp.dot(h1, w2_ref[...], preferred_element_type=F32)

    @pl.when(f == pl.num_programs(1) - 1)
    def _():
        y = x_ref[...] + acc_ref[...]
        if final_norm:
            y = _rms_rows(y, fg_ref[...])
        o_ref[...] = y


def _ffn(x, g, w1, w2, layer, final_g, final_norm, tm=512, tf=1024):
    n, d = x.shape
    dff = w1.shape[2]
    return pl.pallas_call(
        functools.partial(_ffn_kernel, final_norm=final_norm),
        grid=(n // tm, dff // tf),
        in_specs=[
            pl.BlockSpec((tm, d), lambda i, f: (i, 0)),
            pl.BlockSpec((1, d), lambda i, f: (0, 0)),
            pl.BlockSpec((None, d, tf), lambda i, f: (layer, 0, f)),
            pl.BlockSpec((None, tf, d), lambda i, f: (layer, f, 0)),
            pl.BlockSpec((1, d), lambda i, f: (0, 0)),
        ],
        out_specs=pl.BlockSpec((tm, d), lambda i, f: (i, 0)),
        out_shape=jax.ShapeDtypeStruct((n, d), F32),
        scratch_shapes=[pltpu.VMEM((tm, d), BF16), pltpu.VMEM((tm, d), F32)],
        compiler_params=_params("parallel", "arbitrary"),
        name="ffn",
    )(x, g.reshape(1, d), w1, w2, final_g.reshape(1, d))


def _gelu(x):
    return 0.5 * x * (1.0 + lax.erf(x * math.sqrt(0.5)))


def _sgu_kernel(u_ref, v_ref, lng_ref, w_ref, bt_ref, o_ref):
    t_i = lax.broadcasted_iota(jnp.int32, (A_CHUNK, A_CHUNK), 0)
    s_i = lax.broadcasted_iota(jnp.int32, (A_CHUNK, A_CHUNK), 1)
    causal = s_i <= t_i
    n_sub = u_ref.shape[0] // A_CHUNK
    for g in range(A_GROUPS):
        sl = slice(g * A_DIM, (g + 1) * A_DIM)
        w = jnp.where(causal, w_ref[g], 0.0).astype(BF16)
        for sub in range(n_sub):
            rsl = slice(sub * A_CHUNK, (sub + 1) * A_CHUNK)
            v = _gelu(v_ref[rsl, sl])
            mu = jnp.mean(v, axis=-1, keepdims=True)
            vc = v - mu
            var = jnp.mean(vc * vc, axis=-1, keepdims=True)
            vn = vc * lax.rsqrt(var + EPS) * lng_ref[g:g + 1, :]
            z = jnp.dot(w, vn.astype(BF16), preferred_element_type=F32) + bt_ref[:, g:g + 1]
            o_ref[rsl, sl] = (_gelu(u_ref[rsl, sl]) * z).astype(o_ref.dtype)


def _sgu(p_main, ln_g, w_s, b_s_t, chunks_per_step=4):
    n = p_main.shape[0]
    rows = A_CHUNK * chunks_per_step
    return pl.pallas_call(
        _sgu_kernel,
        grid=(n // rows,),
        in_specs=[
            pl.BlockSpec((rows, A_WIDTH), lambda i: (i, 0)),
            pl.BlockSpec((rows, A_WIDTH), lambda i: (i, 1)),
            pl.BlockSpec((A_GROUPS, A_DIM), lambda i: (0, 0)),
            pl.BlockSpec((A_GROUPS, A_CHUNK, A_CHUNK), lambda i: (0, 0, 0)),
            pl.BlockSpec((A_CHUNK, A_GROUPS), lambda i: (0, 0)),
        ],
        out_specs=pl.BlockSpec((rows, A_WIDTH), lambda i: (i, 0)),
        out_shape=jax.ShapeDtypeStruct((n, A_WIDTH), BF16),
        compiler_params=_params("parallel"),
        name="sgu",
    )(p_main, p_main, ln_g, w_s, b_s_t)


def _cumsum_rows(x):
    n = x.shape[0]
    row = lax.broadcasted_iota(jnp.int32, x.shape, 0)
    sh = 1
    while sh < n:
        x = x + jnp.where(row >= sh, pltpu.roll(x, sh, axis=0), 0.0)
        sh *= 2
    return x


def _log_sigmoid(x):
    return jnp.minimum(x, 0.0) - jnp.log1p(jnp.exp(-jnp.abs(x)))


def _gla_kernel(q_ref, k_ref, v_ref, r_ref, g_ref, w2_ref, gb_ref, ng_ref, o_ref, s_ref):
    @pl.when(pl.program_id(1) == 0)
    def _():
        s_ref[...] = jnp.zeros_like(s_ref)

    c = B_CHUNK
    n_sub = q_ref.shape[0] // c
    gate = jnp.dot(g_ref[...].astype(BF16), w2_ref[...], preferred_element_type=F32) + gb_ref[...]
    log_a = _log_sigmoid(gate) * (1.0 / B_TAU)
    i_i = lax.broadcasted_iota(jnp.int32, (c, c), 0)
    j_i = lax.broadcasted_iota(jnp.int32, (c, c), 1)
    causal = j_i <= i_i
    for h in range(B_HEADS):
        ksl = slice(h * B_DK, (h + 1) * B_DK)
        vsl = slice(h * B_DV, (h + 1) * B_DV)
        state = s_ref[h]
        for sub in range(n_sub):
            rsl = slice(sub * c, (sub + 1) * c)
            cum = _cumsum_rows(log_a[rsl, ksl])
            last = cum[c - 1:c, :]
            ref = 0.5 * last
            q = q_ref[rsl, ksl] * (B_DK ** -0.5)
            k = k_ref[rsl, ksl]
            v = v_ref[rsl, vsl].astype(BF16)
            qe = (q * jnp.exp(cum - ref)).astype(BF16)
            ke = (k * jnp.exp(ref - cum)).astype(BF16)
            scores = lax.dot_general(qe, ke, (((1,), (1,)), ((), ())), preferred_element_type=F32)
            scores = jnp.where(causal, scores, 0.0).astype(BF16)
            o = jnp.dot(scores, v, preferred_element_type=F32)
            q_inter = (q * jnp.exp(cum)).astype(BF16)
            o = o + jnp.dot(q_inter, state.astype(BF16), preferred_element_type=F32)
            cum_t = cum.T
            last_t = cum_t[:, c - 1:c]
            k_state_t = (k.T * jnp.exp(last_t - cum_t)).astype(BF16)
            state = jnp.exp(last_t) * state + jnp.dot(k_state_t, v, preferred_element_type=F32)
            o = _rms_rows(o, ng_ref[...])
            r = r_ref[rsl, vsl]
            o_ref[rsl, vsl] = (o * (r * jax.nn.sigmoid(r))).astype(o_ref.dtype)
        s_ref[h] = state


def _gla(p_main, p_extra, gate_w2_pad, gate_b, norm_g, bsz, seq, chunks_per_step=4):
    n = p_main.shape[0]
    c = B_CHUNK * chunks_per_step
    nc = seq // c
    qk_w = B_HEADS * B_DK
    row = lambda b, i: b * nc + i
    return pl.pallas_call(
        _gla_kernel,
        grid=(bsz, nc),
        in_specs=[
            pl.BlockSpec((c, qk_w), lambda b, i: (row(b, i), (2 * A_WIDTH) // qk_w)),
            pl.BlockSpec((c, qk_w), lambda b, i: (row(b, i), (2 * A_WIDTH) // qk_w + 1)),
            pl.BlockSpec((c, B_WIDTH), lambda b, i: (row(b, i), (2 * A_WIDTH + 2 * qk_w) // B_WIDTH)),
            pl.BlockSpec((c, B_WIDTH), lambda b, i: (row(b, i), (2 * A_WIDTH + 2 * qk_w) // B_WIDTH + 1)),
            pl.BlockSpec((c, LANES), lambda b, i: (row(b, i), 0)),
            pl.BlockSpec((LANES, qk_w), lambda b, i: (0, 0)),
            pl.BlockSpec((1, qk_w), lambda b, i: (0, 0)),
            pl.BlockSpec((1, B_DV), lambda b, i: (0, 0)),
        ],
        out_specs=pl.BlockSpec((c, B_WIDTH), lambda b, i: (row(b, i), 0)),
        out_shape=jax.ShapeDtypeStruct((n, B_WIDTH), BF16),
        scratch_shapes=[pltpu.VMEM((B_HEADS, B_DK, B_DV), F32)],
        compiler_params=_params("parallel", "arbitrary"),
        name="gla",
    )(p_main, p_main, p_main, p_main, p_extra, gate_w2_pad, gate_b.reshape(1, qk_w),
      norm_g.reshape(1, B_DV))


def _t5_bucket(dist):
    max_exact = REL_BUCKETS // 2
    d = jnp.maximum(dist, 1).astype(F32)
    large = max_exact + (jnp.log(d / max_exact) / math.log(REL_MAX_DIST / max_exact)
                         * (REL_BUCKETS - max_exact)).astype(jnp.int32)
    large = jnp.minimum(large, REL_BUCKETS - 1)
    return jnp.where(dist < max_exact, dist, large)


def _bias_kernel(rb_ref, o_ref):
    s_i = lax.broadcasted_iota(jnp.int32, (C_QBLOCK, C_QBLOCK), 0)
    t_i = lax.broadcasted_iota(jnp.int32, (C_QBLOCK, C_QBLOCK), 1)
    for delta in range(2):
        bucket = _t5_bucket(jnp.maximum(delta * C_QBLOCK + t_i - s_i, 0))
        for h in range(C_HEADS):
            acc = jnp.zeros((C_QBLOCK, C_QBLOCK), F32)
            for b in range(REL_BUCKETS):
                acc = jnp.where(bucket == b, rb_ref[b, h], acc)
            o_ref[h, delta] = acc


def _bias_tiles(rel_bias):
    return pl.pallas_call(
        _bias_kernel,
        in_specs=[pl.BlockSpec(memory_space=pltpu.SMEM)],
        out_specs=pl.BlockSpec(memory_space=pltpu.VMEM),
        out_shape=jax.ShapeDtypeStruct((C_HEADS, 2, C_QBLOCK, C_QBLOCK), F32),
        name="bias_tiles",
    )(rel_bias)


def _dsa_kernel(rb_ref, q_ref, kv_ref, iq_ref, ex_ref, bias_ref, hsel_ref, o_ref,
                keys_ref, mask_ref, vt_ref, vtw_ref, iklo_ref, ikhi_ref, iqa_ref, qg_ref,
                acc_ref, m_ref, bvec_ref, *, seq, ksel):
    qb = pl.program_id(1)
    blk = C_QBLOCK
    nb = seq // blk
    n_pair = C_IDX_HEADS // 2
    n_sel = lax.shift_right_logical(qb, 2) + 1
    sel_tiles = C_SEL_ROWS // SUBLANES
    lane = lax.broadcasted_iota(jnp.int32, (blk, LANES), 1)
    t_row = lax.broadcasted_iota(jnp.int32, (1, blk), 1)
    d_aug = C_HEAD_DIM + C_VPAD

    def ones_row(width):
        r = lax.broadcasted_iota(jnp.int32, (C_VPAD, width), 0)
        return jnp.where(r == 0, 1.0, 0.0).astype(BF16)

    @pl.when(qb == 0)
    def _():
        def fill(j, k_sq):
            rows = pl.ds(pl.multiple_of(j * blk, blk), blk)
            ex = ex_ref[rows, :]
            iklo_ref[rows, :] = jnp.where(lane < C_IDX_DIM, ex, 0.0).astype(BF16)
            ikhi_ref[rows, :] = jnp.where(lane >= C_IDX_DIM, pltpu.roll(ex, C_IDX_DIM, axis=1),
                                          0.0).astype(BF16)
            for kvh in range(C_KV_HEADS):
                vblk = kv_ref[rows, C_KVW + kvh * C_HEAD_DIM:C_KVW + (kvh + 1) * C_HEAD_DIM]
                vt_ref[kvh, j, 0:C_HEAD_DIM, :] = vblk.astype(F32).T.astype(BF16)
                vt_ref[kvh, j, C_HEAD_DIM:d_aug, :] = ones_row(blk)
            k_sq_new = []
            for kvh in range(C_KV_HEADS):
                kblk = kv_ref[rows, kvh * C_HEAD_DIM:(kvh + 1) * C_HEAD_DIM].astype(F32)
                k_sq_new.append(jnp.maximum(k_sq[kvh], jnp.sum(kblk * kblk, axis=1, keepdims=True)))
            return tuple(k_sq_new)
        k_sq = lax.fori_loop(0, nb, fill, tuple(jnp.zeros((blk, 1), F32) for _ in range(C_KV_HEADS)))
        lane_row = lax.broadcasted_iota(jnp.int32, (1, LANES), 1)
        kn_row = jnp.zeros((1, LANES), F32)
        b_row = jnp.zeros((1, LANES), F32)
        for kvh in range(C_KV_HEADS):
            kn_row = jnp.where(lax.div(lane_row, C_GROUP) == kvh, jnp.max(jnp.sqrt(k_sq[kvh])), kn_row)
        for h in range(C_HEADS):
            b_abs = jnp.abs(rb_ref[0, h])
            for b in range(1, REL_BUCKETS):
                b_abs = jnp.maximum(b_abs, jnp.abs(rb_ref[b, h]))
            b_row = jnp.where(lane_row == h, b_abs * math.log2(math.e), b_row)
        bvec_ref[0:1, :] = kn_row
        bvec_ref[1:2, :] = b_row

        def fill_wide(i, carry):
            rows = pl.ds(pl.multiple_of(i * C_ATT_ROWS, C_ATT_ROWS), C_ATT_ROWS)
            for kvh in range(C_KV_HEADS):
                vblk = kv_ref[rows, C_KVW + kvh * C_HEAD_DIM:C_KVW + (kvh + 1) * C_HEAD_DIM]
                vtw_ref[kvh, i, 0:C_HEAD_DIM, :] = vblk.astype(F32).T.astype(BF16)
                vtw_ref[kvh, i, C_HEAD_DIM:d_aug, :] = ones_row(C_ATT_ROWS)
            return carry
        lax.fori_loop(0, seq // C_ATT_ROWS, fill_wide, 0)

    q_rows = pl.ds(pl.multiple_of(qb * blk, blk), blk)
    ext_t = ex_ref[q_rows, :].T
    idx_scale = (C_IDX_DIM ** -0.5) * (C_IDX_HEADS ** -0.5)
    w_idx = [ext_t[C_IDX_DIM + h:C_IDX_DIM + h + 1, :] * idx_scale for h in range(C_IDX_HEADS)]
    for p in range(n_pair):
        iqa_ref[p * blk:(p + 1) * blk, :] = iq_ref[:, p * LANES:(p + 1) * LANES]
    for kvh in range(C_KV_HEADS):
        for g in range(C_GROUP):
            h = kvh * C_GROUP + g
            qg_ref[kvh, g * blk:(g + 1) * blk, :] = q_ref[:, h * C_HEAD_DIM:(h + 1) * C_HEAD_DIM]

    s_iu = lax.broadcasted_iota(jnp.int32, (C_IDX_ROWS, blk), 0)
    t_iu = lax.broadcasted_iota(jnp.int32, (C_IDX_ROWS, blk), 1)
    nt = (((1,), (1,)), ((), ()))

    def idx_body(u, carry):
        mx, mn = carry
        rows = pl.ds(pl.multiple_of(u * C_IDX_ROWS, C_IDX_ROWS), C_IDX_ROWS)
        ik_lo = iklo_ref[rows, :]
        ik_hi = ikhi_ref[rows, :]
        acc = jnp.zeros((C_IDX_ROWS, blk), F32)
        for c in range(n_pair // 2):
            w = iqa_ref[c * 2 * blk:(c + 1) * 2 * blk, :]
            sc_even = lax.dot_general(ik_lo, w, nt, preferred_element_type=F32)
            sc_odd = lax.dot_general(ik_hi, w, nt, preferred_element_type=F32)
            for pp in range(2):
                p = 2 * c + pp
                sl = slice(pp * blk, (pp + 1) * blk)
                acc = acc + jnp.maximum(sc_even[:, sl], 0.0) * w_idx[2 * p]
                acc = acc + jnp.maximum(sc_odd[:, sl], 0.0) * w_idx[2 * p + 1]
        causal = (u * C_IDX_ROWS + s_iu) <= (qb * blk + t_iu)
        keys_ref[rows, :] = jnp.where(causal, acc, -jnp.inf)
        mx = jnp.maximum(mx, jnp.max(jnp.where(causal, acc, -jnp.inf), axis=0, keepdims=True))
        mn = jnp.minimum(mn, jnp.min(jnp.where(causal, acc, jnp.inf), axis=0, keepdims=True))
        return mx, mn

    mx, mn = lax.fori_loop(0, n_sel * (C_SEL_ROWS // C_IDX_ROWS), idx_body,
                           (jnp.full((1, blk), -jnp.inf, F32), jnp.full((1, blk), jnp.inf, F32)))

    def sel_rows(i):
        return pl.ds(pl.multiple_of(i * C_SEL_ROWS, C_SEL_ROWS), C_SEL_ROWS)

    def tree_sum(x):
        parts = [x[i] for i in range(x.shape[0])]
        while len(parts) > 1:
            parts = [parts[i] + parts[i + 1] for i in range(0, len(parts), 2)]
        return parts[0]

    n_lanes_acc = 8

    def count(pred):
        def body(i, cs):
            kb = keys_ref[sel_rows(i), :].reshape(sel_tiles, SUBLANES, blk)
            hit = pred(kb, i)
            cs = list(cs)
            for tile in range(sel_tiles):
                a = tile % n_lanes_acc
                cs[a] = jnp.where(hit[tile], cs[a] + 1.0, cs[a])
            return tuple(cs)
        cs = lax.fori_loop(0, n_sel, body, tuple(jnp.zeros((SUBLANES, blk), F32) for _ in range(n_lanes_acc)))
        return jnp.sum(tree_sum(jnp.stack(cs)), axis=0, keepdims=True)

    n_causal = (qb * blk + t_row + 1).astype(F32)
    k_eff = jnp.minimum(n_causal, float(ksel))
    lo0 = mn
    hi0 = mx + (jnp.abs(mx) * 1e-6 + 1e-30)
    cnt0 = n_causal

    def active_of(lo, hi, cnt):
        mid = lo + 0.5 * (hi - lo)
        collapsed = (mid <= lo) | (mid >= hi)
        return mid, (cnt > k_eff) & jnp.logical_not(collapsed)

    def bis_cond(st):
        return st[3] > 0.0

    def bis_step(_, st):
        lo, hi, cnt = st
        mid, active = active_of(lo, hi, cnt)
        c = count(lambda kb, i: kb >= mid)
        ge = c >= k_eff
        return (jnp.where(active & ge, mid, lo), jnp.where(active & jnp.logical_not(ge), mid, hi),
                jnp.where(active & ge, c, cnt))

    def n_active(st):
        _, active = active_of(*st)
        return jnp.sum(jnp.where(active, 1.0, 0.0))

    def bis_body(st):
        st = lax.fori_loop(0, 2, bis_step, st[:3])
        return (*st, n_active(st))

    n_warm = jnp.where(qb * blk >= ksel, C_BISECT_WARMUP, 0)
    st = lax.fori_loop(0, n_warm, bis_step, (lo0, hi0, cnt0))
    thr, _, cnt, _ = lax.while_loop(bis_cond, bis_body, (*st, n_active(st)))

    def mask_body(i, carry):
        mask_ref[sel_rows(i), :] = jnp.where(keys_ref[sel_rows(i), :] >= thr, 0.0, MASK_NEG)
        return carry
    lax.fori_loop(0, n_sel, mask_body, 0)

    tie = cnt > k_eff

    @pl.when(jnp.sum(jnp.where(tie, 1.0, 0.0)) > 0.0)
    def _():
        tile_i = lax.broadcasted_iota(jnp.int32, (sel_tiles, SUBLANES, blk), 0)
        sub_i = lax.broadcasted_iota(jnp.int32, (sel_tiles, SUBLANES, blk), 1)
        s_i2 = lax.broadcasted_iota(jnp.int32, (C_SEL_ROWS, blk), 0)

        def pos3(i):
            return (i * C_SEL_ROWS + tile_i * SUBLANES + sub_i).astype(F32)

        need = k_eff - count(lambda kb, i: kb > thr)
        lo_i = jnp.full((1, blk), -1.0, F32)
        hi_i = jnp.zeros((1, blk), F32) + (n_sel * C_SEL_ROWS - 1).astype(F32)
        n_steps = int(math.ceil(math.log2(seq))) + 1

        def tb(_, st):
            lo_i, hi_i = st
            mid = jnp.floor((lo_i + hi_i) * 0.5)
            c = count(lambda kb, i: (kb == thr) & (pos3(i) <= mid))
            ge = c >= need
            return jnp.where(ge, lo_i, mid), jnp.where(ge, mid, hi_i)
        _, cut = lax.fori_loop(0, n_steps, tb, (lo_i, hi_i))

        def fix_body(i, carry):
            kb = keys_ref[sel_rows(i), :]
            pos = (i * C_SEL_ROWS + s_i2).astype(F32)
            sel = (kb > thr) | ((kb == thr) & (pos <= cut))
            mask_ref[sel_rows(i), :] = jnp.where(tie, jnp.where(sel, 0.0, MASK_NEG), mask_ref[sel_rows(i), :])
            return carry
        lax.fori_loop(0, n_sel, fix_body, 0)

    log2e = math.log2(math.e)
    n_far = jnp.maximum(qb - 1, 0)
    n_far_wide = lax.div(n_far, C_ATT_ROWS // blk)

    def qk_logits(rows):
        return [lax.dot_general(kv_ref[rows, kvh * C_HEAD_DIM:(kvh + 1) * C_HEAD_DIM], qg_ref[kvh], nt,
                                preferred_element_type=F32) for kvh in range(C_KV_HEADS)]

    def far_bias(h):
        return rb_ref[REL_BUCKETS - 1, h] * log2e

    def attend_online(row0, n_rows, near_bias, v_tile):
        rows = pl.ds(pl.multiple_of(row0, blk), n_rows)
        madd = mask_ref[rows, :]
        logits = qk_logits(rows)
        results = []
        for kvh in range(C_KV_HEADS):
            lg = logits[kvh]
            m_old = m_ref[kvh]
            ys, m_parts, shifts = [], [], []
            for g in range(C_GROUP):
                h = kvh * C_GROUP + g
                sl = slice(g * blk, (g + 1) * blk)
                if near_bias is None:
                    y = lg[:, sl] + madd
                    m_g = jnp.maximum(m_old[:, sl], jnp.max(y, axis=0, keepdims=True) + far_bias(h))
                    shifts.append(m_g - far_bias(h))
                else:
                    y = lg[:, sl] + (madd + near_bias(h) * log2e)
                    m_g = jnp.maximum(m_old[:, sl], jnp.max(y, axis=0, keepdims=True))
                    shifts.append(m_g)
                ys.append(y)
                m_parts.append(m_g)
            m_new = jnp.concatenate(m_parts, axis=1)
            alpha = jnp.exp2(m_old - m_new)
            p = jnp.concatenate([jnp.exp2(ys[g] - shifts[g]) for g in range(C_GROUP)], axis=1)
            pv = jnp.dot(v_tile(kvh), p.astype(BF16), preferred_element_type=F32)
            results.append((m_new, acc_ref[kvh] * alpha + pv))
        for kvh, (m_new, acc_new) in enumerate(results):
            m_ref[kvh] = m_new
            acc_ref[kvh] = acc_new

    def attend_direct(row0, n_rows, near_bias, v_tile):
        rows = pl.ds(pl.multiple_of(row0, blk), n_rows)
        madd = mask_ref[rows, :]
        logits = qk_logits(rows)
        results = []
        for kvh in range(C_KV_HEADS):
            lg = logits[kvh]
            parts = []
            for g in range(C_GROUP):
                h = kvh * C_GROUP + g
                sl = slice(g * blk, (g + 1) * blk)
                if near_bias is None:
                    parts.append(jnp.exp2(lg[:, sl] + madd))
                else:
                    parts.append(jnp.exp2(lg[:, sl] + (madd + near_bias(h) * log2e)))
            p = jnp.concatenate(parts, axis=1).astype(BF16)
            results.append(acc_ref[kvh] + jnp.dot(v_tile(kvh), p, preferred_element_type=F32))
        for kvh, acc_new in enumerate(results):
            acc_ref[kvh] = acc_new

    def apply_far_bias():
        for kvh in range(C_KV_HEADS):
            row = jnp.concatenate([jnp.exp2(jnp.full((1, blk), far_bias(kvh * C_GROUP + g), F32))
                                   for g in range(C_GROUP)], axis=1)
            acc_ref[kvh] = acc_ref[kvh] * row

    def run_attention(attend, after_far):
        acc_ref[...] = jnp.zeros_like(acc_ref)

        def one_block(j):
            return lambda kvh: vt_ref[kvh, j]

        def two_blocks(j):
            return lambda kvh: jnp.concatenate([vt_ref[kvh, j], vt_ref[kvh, j + 1]], axis=1)

        def far_wide_body(i, carry):
            attend(i * C_ATT_ROWS, C_ATT_ROWS, None, lambda kvh: vtw_ref[kvh, i])
            return carry
        lax.fori_loop(0, n_far_wide, far_wide_body, 0)
        j_left = n_far_wide * (C_ATT_ROWS // blk)
        n_left = n_far - j_left

        @pl.when(n_left >= 2)
        def _():
            attend(j_left * blk, 2 * blk, None, two_blocks(j_left))

        @pl.when(jnp.bitwise_and(n_left, 1) == 1)
        def _():
            j = j_left + n_left - 1
            attend(j * blk, blk, None, one_block(j))
        after_far()

        @pl.when(qb >= 1)
        def _():
            attend((qb - 1) * blk, 2 * blk,
                   lambda h: jnp.concatenate([bias_ref[h, 1], bias_ref[h, 0]], axis=0), two_blocks(qb - 1))

        @pl.when(qb == 0)
        def _():
            attend(0, blk, lambda h: bias_ref[h, 0], one_block(0))

    q_all = q_ref[...].astype(F32)
    q_sq = jnp.dot((q_all * q_all).astype(BF16), hsel_ref[...], preferred_element_type=F32)
    bound = jnp.sqrt(q_sq) * bvec_ref[0:1, :] + bvec_ref[1:2, :]
    direct_ok = jnp.max(bound) * 1.05 < C_DIRECT_EXP_LIMIT

    @pl.when(direct_ok)
    def _():
        run_attention(attend_direct, apply_far_bias)

    @pl.when(jnp.logical_not(direct_ok))
    def _():
        m_ref[...] = jnp.full(m_ref.shape, MASK_NEG, F32)
        run_attention(attend_online, lambda: None)

    for kvh in range(C_KV_HEADS):
        o_t = acc_ref[kvh, 0:C_HEAD_DIM, :] / acc_ref[kvh, C_HEAD_DIM:C_HEAD_DIM + 1, :]
        for g in range(C_GROUP):
            h = kvh * C_GROUP + g
            o_ref[:, h * C_HEAD_DIM:(h + 1) * C_HEAD_DIM] = o_t[:, g * blk:(g + 1) * blk].T.astype(o_ref.dtype)


def _dsa(p_main, p_extra, bias_tiles, rel_bias, bsz, seq):
    assert seq % C_SEL_ROWS == 0 and seq % C_ATT_ROWS == 0
    n = p_main.shape[0]
    nb = seq // C_QBLOCK
    blk = C_QBLOCK
    ksel = min(C_TOPK_MAX, seq // 4)
    n_pair = C_IDX_HEADS // 2
    d_aug = C_HEAD_DIM + C_VPAD
    head_sel = jnp.repeat(jnp.eye(C_HEADS, LANES, dtype=BF16), C_HEAD_DIM, axis=0)
    row = lambda b, i: b * nb + i
    return pl.pallas_call(
        functools.partial(_dsa_kernel, seq=seq, ksel=ksel),
        grid=(bsz, nb),
        in_specs=[
            pl.BlockSpec(memory_space=pltpu.SMEM),
            pl.BlockSpec((blk, C_WIDTH), lambda b, i: (row(b, i), 0)),
            pl.BlockSpec((seq, 2 * C_KVW), lambda b, i: (b, C_WIDTH // (2 * C_KVW))),
            pl.BlockSpec((blk, C_IDX_HEADS * C_IDX_DIM),
                         lambda b, i: (row(b, i), (C_WIDTH + 2 * C_KVW) // (C_IDX_HEADS * C_IDX_DIM))),
            pl.BlockSpec((seq, LANES), lambda b, i: (b, 0)),
            pl.BlockSpec((C_HEADS, 2, blk, blk), lambda b, i: (0, 0, 0, 0)),
            pl.BlockSpec((C_WIDTH, LANES), lambda b, i: (0, 0)),
        ],
        out_specs=pl.BlockSpec((blk, C_WIDTH), lambda b, i: (row(b, i), 0)),
        out_shape=jax.ShapeDtypeStruct((n, C_WIDTH), BF16),
        scratch_shapes=[
            pltpu.VMEM((seq, blk), F32),
            pltpu.VMEM((seq, blk), F32),
            pltpu.VMEM((C_KV_HEADS, nb, d_aug, blk), BF16),
            pltpu.VMEM((C_KV_HEADS, seq // C_ATT_ROWS, d_aug, C_ATT_ROWS), BF16),
            pltpu.VMEM((seq, LANES), BF16),
            pltpu.VMEM((seq, LANES), BF16),
            pltpu.VMEM((n_pair * blk, LANES), BF16),
            pltpu.VMEM((C_KV_HEADS, C_GROUP * blk, C_HEAD_DIM), BF16),
            pltpu.VMEM((C_KV_HEADS, d_aug, C_GROUP * blk), F32),
            pltpu.VMEM((C_KV_HEADS, 1, C_GROUP * blk), F32),
            pltpu.VMEM((SUBLANES, LANES), F32),
        ],
        compiler_params=_params("parallel", "arbitrary"),
        name="dsa",
    )(rel_bias, p_main, p_main, p_main, p_extra, bias_tiles, head_sel)


AB_MAIN = 2 * A_WIDTH + 2 * B_HEADS * B_DK + 2 * B_WIDTH
C_MAIN = C_WIDTH + 2 * C_KVW + C_IDX_HEADS * C_IDX_DIM


def _extra_cols(w_in_all, n_main):
    n_extra = w_in_all.shape[2] - n_main
    return jnp.pad(w_in_all[:, :, n_main:], ((0, 0), (0, 0), (0, LANES - n_extra))).astype(BF16)


def _even_mixer(h, g, w_in, w_extra, li, v_ln_g, w_s, b_s, gate_w2, gate_b, out_norm_g, w_out, bsz, seq):
    p_main, p_extra = _norm_matmul(h, g, w_in, w_extra, li, AB_MAIN, F32)
    a_out = _sgu(p_main, v_ln_g, w_s, b_s.T)
    gate_w2_pad = jnp.pad(gate_w2, ((0, LANES - B_RANK), (0, 0))).astype(BF16)
    b_out = _gla(p_main, p_extra, gate_w2_pad, gate_b, out_norm_g, bsz, seq)
    return _proj_residual(h, [a_out, b_out], w_out, li)


def _odd_mixer(h, g, w_in, w_extra, li, w_out, bias_tiles, rel_bias, bsz, seq):
    p_main, p_extra = _norm_matmul(h, g, w_in, w_extra, li, C_MAIN, BF16)
    o = _dsa(p_main, p_extra, bias_tiles, rel_bias, bsz, seq)
    return _proj_residual(h, [o], w_out, li)


def _scaled_c_in(c_w_in):
    q_scale = (C_HEAD_DIM ** -0.5) * math.log2(math.e)
    col_scale = jnp.where(jnp.arange(c_w_in.shape[2]) < C_WIDTH, q_scale, 1.0).astype(F32)
    return (c_w_in * col_scale).astype(BF16)


def kernel(x, norm_mix_g, norm_ffn_g, final_norm_g, ab_w_in, a_v_ln_g, a_w_s, a_b_s, b_gate_w2,
           b_gate_b, b_out_norm_g, ab_w_out, c_w_in, c_w_out, rel_bias, ffn_w1, ffn_w2):
    bsz, seq, d = x.shape
    depth = norm_mix_g.shape[0]
    h = x.reshape(bsz * seq, d)
    bias_tiles = _bias_tiles(rel_bias)
    ab_in, ab_in_extra, ab_out = ab_w_in.astype(BF16), _extra_cols(ab_w_in, AB_MAIN), ab_w_out.astype(BF16)
    c_in, c_in_extra, c_out = _scaled_c_in(c_w_in), _extra_cols(c_w_in, C_MAIN), c_w_out.astype(BF16)
    w1, w2 = ffn_w1.astype(BF16), ffn_w2.astype(BF16)
    for layer in range(depth):
        i = layer // 2
        if layer % 2 == 0:
            h = _even_mixer(h, norm_mix_g[layer], ab_in, ab_in_extra, i, a_v_ln_g[i], a_w_s[i], a_b_s[i],
                            b_gate_w2[i], b_gate_b[i], b_out_norm_g[i], ab_out, bsz, seq)
        else:
            h = _odd_mixer(h, norm_mix_g[layer], c_in, c_in_extra, i, c_out, bias_tiles, rel_bias, bsz, seq)
        h = _ffn(h, norm_ffn_g[layer], w1, w2, layer, final_norm_g, layer == depth - 1)
    return h.reshape(bsz, seq, d)
```

```python
import functools
import math

import jax
import jax.numpy as jnp
from jax import lax
from jax.experimental import pallas as pl
from jax.experimental.pallas import tpu as pltpu

F32 = jnp.float32
BF16 = jnp.bfloat16
EPS = 1e-6

VMEM_LIMIT_BYTES = 56 * 1024 * 1024
LANES = 128
SUBLANES = 8

A_GROUPS = 8
A_DIM = 128
A_CHUNK = 128
A_WIDTH = A_GROUPS * A_DIM
B_HEADS = 4
B_DK = 128
B_DV = 256
B_RANK = 16
B_TAU = 16.0
B_CHUNK = 64
B_WIDTH = B_HEADS * B_DV
C_HEADS = 16
C_KV_HEADS = 4
C_GROUP = C_HEADS // C_KV_HEADS
C_HEAD_DIM = 128
C_IDX_HEADS = 16
C_IDX_DIM = 64
C_TOPK_MAX = 256
C_QBLOCK = 128
C_WIDTH = C_HEADS * C_HEAD_DIM
C_KVW = C_KV_HEADS * C_HEAD_DIM
C_SEL_ROWS = 4 * C_QBLOCK
C_IDX_ROWS = 4 * C_QBLOCK
C_ATT_ROWS = 4 * C_QBLOCK
C_VPAD = 16
C_BISECT_WARMUP = 16
C_DIRECT_EXP_LIMIT = 60.0
REL_BUCKETS = 32
REL_MAX_DIST = 128
MASK_NEG = -1e30


def _params(*sem):
    return pltpu.CompilerParams(dimension_semantics=sem, vmem_limit_bytes=VMEM_LIMIT_BYTES)


def _rms_rows(x, g):
    ms = jnp.mean(x * x, axis=-1, keepdims=True)
    return x * lax.rsqrt(ms + EPS) * g


def _norm_matmul_kernel(x_ref, g_ref, w_ref, we_ref, o_ref, oe_ref, xn_ref):
    @pl.when(pl.program_id(1) == 0)
    def _():
        xn = _rms_rows(x_ref[...], g_ref[...]).astype(BF16)
        xn_ref[...] = xn
        oe_ref[...] = jnp.dot(xn, we_ref[...], preferred_element_type=F32)

    o_ref[...] = jnp.dot(xn_ref[...], w_ref[...], preferred_element_type=F32).astype(o_ref.dtype)


def _norm_matmul(x, g, w_all, w_extra, li, nm, out_dtype, tm=1024, tn=1024):
    n, d = x.shape
    assert nm % tn == 0 and nm <= w_all.shape[2]
    return pl.pallas_call(
        _norm_matmul_kernel,
        grid=(n // tm, nm // tn),
        in_specs=[
            pl.BlockSpec((tm, d), lambda i, j: (i, 0)),
            pl.BlockSpec((1, d), lambda i, j: (0, 0)),
            pl.BlockSpec((None, d, tn), lambda i, j: (li, 0, j)),
            pl.BlockSpec((None, d, LANES), lambda i, j: (li, 0, 0)),
        ],
        out_specs=[
            pl.BlockSpec((tm, tn), lambda i, j: (i, j)),
            pl.BlockSpec((tm, LANES), lambda i, j: (i, 0)),
        ],
        out_shape=[jax.ShapeDtypeStruct((n, nm), out_dtype), jax.ShapeDtypeStruct((n, LANES), F32)],
        scratch_shapes=[pltpu.VMEM((tm, d), BF16)],
        compiler_params=_params("parallel", "arbitrary"),
        name="norm_matmul",
    )(x, g.reshape(1, d), w_all, w_extra)


def _proj_res_kernel(*refs, n_in):
    h_ref = refs[0]
    o_ref = refs[1 + 2 * n_in]
    acc = h_ref[...]
    for k in range(n_in):
        acc = acc + jnp.dot(refs[1 + k][...], refs[1 + n_in + k][...], preferred_element_type=F32)
    o_ref[...] = acc


def _proj_residual(h, xs, w, li, tm=512, tn=2048):
    n, d = h.shape
    n_in = len(xs)
    kx = xs[0].shape[1]
    assert all(x.shape[1] == kx for x in xs) and kx * n_in == w.shape[1]
    ws = [w] * n_in
    in_specs = [pl.BlockSpec((tm, tn), lambda i, j: (i, j))]
    in_specs += [pl.BlockSpec((tm, kx), lambda i, j: (i, 0)) for _ in xs]
    in_specs += [pl.BlockSpec((None, kx, tn), functools.partial(lambda i, j, k: (li, k, j), k=k))
                 for k in range(n_in)]
    return pl.pallas_call(
        functools.partial(_proj_res_kernel, n_in=n_in),
        grid=(n // tm, d // tn),
        in_specs=in_specs,
        out_specs=pl.BlockSpec((tm, tn), lambda i, j: (i, j)),
        out_shape=jax.ShapeDtypeStruct((n, d), F32),
        compiler_params=_params("parallel", "parallel"),
        name="proj_residual",
    )(h, *xs, *ws)


def _ffn_kernel(x_ref, g_ref, w1_ref, w2_ref, fg_ref, o_ref, xn_ref, acc_ref, *, final_norm):
    f = pl.program_id(1)

    @pl.when(f == 0)
    def _():
        xn_ref[...] = _rms_rows(x_ref[...], g_ref[...]).astype(BF16)
        acc_ref[...] = jnp.zeros_like(acc_ref)

    h1 = jnp.dot(xn_ref[...], w1_ref[...], preferred_element_type=F32)
    h1 = jnp.square(jnp.maximum(h1, 0.0)).astype(BF16)
    acc_ref[...] += jnp.dot(h1, w2_ref[...], preferred_element_type=F32)

    @pl.when(f == pl.num_programs(1) - 1)
    def _():
        y = x_ref[...] + acc_ref[...]
        if final_norm:
            y = _rms_rows(y, fg_ref[...])
        o_ref[...] = y


def _ffn(x, g, w1, w2, layer, final_g, final_norm, tm=512, tf=1024):
    n, d = x.shape
    dff = w1.shape[2]
    return pl.pallas_call(
        functools.partial(_ffn_kernel, final_norm=final_norm),
        grid=(n // tm, dff // tf),
        in_specs=[
            pl.BlockSpec((tm, d), lambda i, f: (i, 0)),
            pl.BlockSpec((1, d), lambda i, f: (0, 0)),
            pl.BlockSpec((None, d, tf), lambda i, f: (layer, 0, f)),
            pl.BlockSpec((None, tf, d), lambda i, f: (layer, f, 0)),
            pl.BlockSpec((1, d), lambda i, f: (0, 0)),
        ],
        out_specs=pl.BlockSpec((tm, d), lambda i, f: (i, 0)),
        out_shape=jax.ShapeDtypeStruct((n, d), F32),
        scratch_shapes=[pltpu.VMEM((tm, d), BF16), pltpu.VMEM((tm, d), F32)],
        compiler_params=_params("parallel", "arbitrary"),
        name="ffn",
    )(x, g.reshape(1, d), w1, w2, final_g.reshape(1, d))


def _gelu(x):
    return 0.5 * x * (1.0 + lax.erf(x * math.sqrt(0.5)))


def _sgu_kernel(u_ref, v_ref, lng_ref, w_ref, bt_ref, o_ref):
    t_i = lax.broadcasted_iota(jnp.int32, (A_CHUNK, A_CHUNK), 0)
    s_i = lax.broadcasted_iota(jnp.int32, (A_CHUNK, A_CHUNK), 1)
    causal = s_i <= t_i
    n_sub = u_ref.shape[0] // A_CHUNK
    for g in range(A_GROUPS):
        sl = slice(g * A_DIM, (g + 1) * A_DIM)
        w = jnp.where(causal, w_ref[g], 0.0).astype(BF16)
        for sub in range(n_sub):
            rsl = slice(sub * A_CHUNK, (sub + 1) * A_CHUNK)
            v = _gelu(v_ref[rsl, sl])
            mu = jnp.mean(v, axis=-1, keepdims=True)
            vc = v - mu
            var = jnp.mean(vc * vc, axis=-1, keepdims=True)
            vn = vc * lax.rsqrt(var + EPS) * lng_ref[g:g + 1, :]
            z = jnp.dot(w, vn.astype(BF16), preferred_element_type=F32) + bt_ref[:, g:g + 1]
            o_ref[rsl, sl] = (_gelu(u_ref[rsl, sl]) * z).astype(o_ref.dtype)


def _sgu(p_main, ln_g, w_s, b_s_t, chunks_per_step=4):
    n = p_main.shape[0]
    rows = A_CHUNK * chunks_per_step
    return pl.pallas_call(
        _sgu_kernel,
        grid=(n // rows,),
        in_specs=[
            pl.BlockSpec((rows, A_WIDTH), lambda i: (i, 0)),
            pl.BlockSpec((rows, A_WIDTH), lambda i: (i, 1)),
            pl.BlockSpec((A_GROUPS, A_DIM), lambda i: (0, 0)),
            pl.BlockSpec((A_GROUPS, A_CHUNK, A_CHUNK), lambda i: (0, 0, 0)),
            pl.BlockSpec((A_CHUNK, A_GROUPS), lambda i: (0, 0)),
        ],
        out_specs=pl.BlockSpec((rows, A_WIDTH), lambda i: (i, 0)),
        out_shape=jax.ShapeDtypeStruct((n, A_WIDTH), BF16),
        compiler_params=_params("parallel"),
        name="sgu",
    )(p_main, p_main, ln_g, w_s, b_s_t)


def _cumsum_rows(x):
    n = x.shape[0]
    row = lax.broadcasted_iota(jnp.int32, x.shape, 0)
    sh = 1
    while sh < n:
        x = x + jnp.where(row >= sh, pltpu.roll(x, sh, axis=0), 0.0)
        sh *= 2
    return x


def _log_sigmoid(x):
    return jnp.minimum(x, 0.0) - jnp.log1p(jnp.exp(-jnp.abs(x)))


def _gla_kernel(q_ref, k_ref, v_ref, r_ref, g_ref, w2_ref, gb_ref, ng_ref, o_ref, s_ref):
    @pl.when(pl.program_id(1) == 0)
    def _():
        s_ref[...] = jnp.zeros_like(s_ref)

    c = B_CHUNK
    n_seq = q_ref.shape[0]
    n_sub = q_ref.shape[1] // c
    i_i = lax.broadcasted_iota(jnp.int32, (c, c), 0)
    j_i = lax.broadcasted_iota(jnp.int32, (c, c), 1)
    causal = j_i <= i_i
    for b in range(n_seq):
        gate = jnp.dot(g_ref[b].astype(BF16), w2_ref[...], preferred_element_type=F32) + gb_ref[...]
        log_a = _log_sigmoid(gate) * (1.0 / B_TAU)
        for h in range(B_HEADS):
            ksl = slice(h * B_DK, (h + 1) * B_DK)
            vsl = slice(h * B_DV, (h + 1) * B_DV)
            state = s_ref[b, h]
            for sub in range(n_sub):
                rsl = slice(sub * c, (sub + 1) * c)
                cum = _cumsum_rows(log_a[rsl, ksl])
                last = cum[c - 1:c, :]
                ref = 0.5 * last
                q = q_ref[b, rsl, ksl] * (B_DK ** -0.5)
                k = k_ref[b, rsl, ksl]
                v = v_ref[b, rsl, vsl].astype(BF16)
                qe = (q * jnp.exp(cum - ref)).astype(BF16)
                ke = (k * jnp.exp(ref - cum)).astype(BF16)
                scores = lax.dot_general(qe, ke, (((1,), (1,)), ((), ())), preferred_element_type=F32)
                scores = jnp.where(causal, scores, 0.0).astype(BF16)
                o = jnp.dot(scores, v, preferred_element_type=F32)
                q_inter = (q * jnp.exp(cum)).astype(BF16)
                o = o + jnp.dot(q_inter, state.astype(BF16), preferred_element_type=F32)
                cum_t = cum.T
                last_t = cum_t[:, c - 1:c]
                k_state_t = (k.T * jnp.exp(last_t - cum_t)).astype(BF16)
                state = jnp.exp(last_t) * state + jnp.dot(k_state_t, v, preferred_element_type=F32)
                o = _rms_rows(o, ng_ref[...])
                r = r_ref[b, rsl, vsl]
                o_ref[b, rsl, vsl] = (o * (r * jax.nn.sigmoid(r))).astype(o_ref.dtype)
            s_ref[b, h] = state


def _gla(p_main, p_extra, gate_w2_pad, gate_b, norm_g, bsz, seq, chunks_per_step=4, seqs_per_step=2):
    n = p_main.shape[0]
    c = B_CHUNK * chunks_per_step
    qk_w = B_HEADS * B_DK
    ns = seqs_per_step
    p3 = p_main.reshape(bsz, seq, p_main.shape[1])
    e3 = p_extra.reshape(bsz, seq, LANES)
    out = pl.pallas_call(
        _gla_kernel,
        grid=(bsz // ns, seq // c),
        in_specs=[
            pl.BlockSpec((ns, c, qk_w), lambda b, i: (b, i, (2 * A_WIDTH) // qk_w)),
            pl.BlockSpec((ns, c, qk_w), lambda b, i: (b, i, (2 * A_WIDTH) // qk_w + 1)),
            pl.BlockSpec((ns, c, B_WIDTH), lambda b, i: (b, i, (2 * A_WIDTH + 2 * qk_w) // B_WIDTH)),
            pl.BlockSpec((ns, c, B_WIDTH), lambda b, i: (b, i, (2 * A_WIDTH + 2 * qk_w) // B_WIDTH + 1)),
            pl.BlockSpec((ns, c, LANES), lambda b, i: (b, i, 0)),
            pl.BlockSpec((LANES, qk_w), lambda b, i: (0, 0)),
            pl.BlockSpec((1, qk_w), lambda b, i: (0, 0)),
            pl.BlockSpec((1, B_DV), lambda b, i: (0, 0)),
        ],
        out_specs=pl.BlockSpec((ns, c, B_WIDTH), lambda b, i: (b, i, 0)),
        out_shape=jax.ShapeDtypeStruct((bsz, seq, B_WIDTH), BF16),
        scratch_shapes=[pltpu.VMEM((ns, B_HEADS, B_DK, B_DV), F32)],
        compiler_params=_params("parallel", "arbitrary"),
        name="gla",
    )(p3, p3, p3, p3, e3, gate_w2_pad, gate_b.reshape(1, qk_w), norm_g.reshape(1, B_DV))
    return out.reshape(n, B_WIDTH)


def _t5_bucket(dist):
    max_exact = REL_BUCKETS // 2
    d = jnp.maximum(dist, 1).astype(F32)
    large = max_exact + (jnp.log(d / max_exact) / math.log(REL_MAX_DIST / max_exact)
                         * (REL_BUCKETS - max_exact)).astype(jnp.int32)
    large = jnp.minimum(large, REL_BUCKETS - 1)
    return jnp.where(dist < max_exact, dist, large)


def _bias_kernel(rb_ref, o_ref):
    s_i = lax.broadcasted_iota(jnp.int32, (C_QBLOCK, C_QBLOCK), 0)
    t_i = lax.broadcasted_iota(jnp.int32, (C_QBLOCK, C_QBLOCK), 1)
    for delta in range(2):
        bucket = _t5_bucket(jnp.maximum(delta * C_QBLOCK + t_i - s_i, 0))
        for h in range(C_HEADS):
            acc = jnp.zeros((C_QBLOCK, C_QBLOCK), F32)
            for b in range(REL_BUCKETS):
                acc = jnp.where(bucket == b, rb_ref[b, h], acc)
            o_ref[h, delta] = acc


def _bias_tiles(rel_bias):
    return pl.pallas_call(
        _bias_kernel,
        in_specs=[pl.BlockSpec(memory_space=pltpu.SMEM)],
        out_specs=pl.BlockSpec(memory_space=pltpu.VMEM),
        out_shape=jax.ShapeDtypeStruct((C_HEADS, 2, C_QBLOCK, C_QBLOCK), F32),
        name="bias_tiles",
    )(rel_bias)


def _dsa_kernel(rb_ref, q_ref, kv_ref, iq_ref, ex_ref, bias_ref, hsel_ref, o_ref,
                keys_ref, mask_ref, vt_ref, vtw_ref, iklo_ref, ikhi_ref, iqa_ref, qg_ref,
                acc_ref, m_ref, bvec_ref, *, seq, ksel):
    qb = pl.program_id(1)
    blk = C_QBLOCK
    nb = seq // blk
    n_pair = C_IDX_HEADS // 2
    n_sel = lax.shift_right_logical(qb, 2) + 1
    sel_tiles = C_SEL_ROWS // SUBLANES
    lane = lax.broadcasted_iota(jnp.int32, (blk, LANES), 1)
    t_row = lax.broadcasted_iota(jnp.int32, (1, blk), 1)
    d_aug = C_HEAD_DIM + C_VPAD

    def ones_row(width):
        r = lax.broadcasted_iota(jnp.int32, (C_VPAD, width), 0)
        return jnp.where(r == 0, 1.0, 0.0).astype(BF16)

    @pl.when(qb == 0)
    def _():
        def fill(j, k_sq):
            rows = pl.ds(pl.multiple_of(j * blk, blk), blk)
            ex = ex_ref[rows, :]
            iklo_ref[rows, :] = jnp.where(lane < C_IDX_DIM, ex, 0.0).astype(BF16)
            ikhi_ref[rows, :] = jnp.where(lane >= C_IDX_DIM, pltpu.roll(ex, C_IDX_DIM, axis=1),
                                          0.0).astype(BF16)
            for kvh in range(C_KV_HEADS):
                vblk = kv_ref[rows, C_KVW + kvh * C_HEAD_DIM:C_KVW + (kvh + 1) * C_HEAD_DIM]
                vt_ref[kvh, j, 0:C_HEAD_DIM, :] = vblk.astype(F32).T.astype(BF16)
                vt_ref[kvh, j, C_HEAD_DIM:d_aug, :] = ones_row(blk)
            k_sq_new = []
            for kvh in range(C_KV_HEADS):
                kblk = kv_ref[rows, kvh * C_HEAD_DIM:(kvh + 1) * C_HEAD_DIM].astype(F32)
                k_sq_new.append(jnp.maximum(k_sq[kvh], jnp.sum(kblk * kblk, axis=1, keepdims=True)))
            return tuple(k_sq_new)
        k_sq = lax.fori_loop(0, nb, fill, tuple(jnp.zeros((blk, 1), F32) for _ in range(C_KV_HEADS)))
        lane_row = lax.broadcasted_iota(jnp.int32, (1, LANES), 1)
        kn_row = jnp.zeros((1, LANES), F32)
        b_row = jnp.zeros((1, LANES), F32)
        for kvh in range(C_KV_HEADS):
            kn_row = jnp.where(lax.div(lane_row, C_GROUP) == kvh, jnp.max(jnp.sqrt(k_sq[kvh])), kn_row)
        for h in range(C_HEADS):
            b_abs = jnp.abs(rb_ref[0, h])
            for b in range(1, REL_BUCKETS):
                b_abs = jnp.maximum(b_abs, jnp.abs(rb_ref[b, h]))
            b_row = jnp.where(lane_row == h, b_abs * math.log2(math.e), b_row)
        bvec_ref[0:1, :] = kn_row
        bvec_ref[1:2, :] = b_row

        def fill_wide(i, carry):
            rows = pl.ds(pl.multiple_of(i * C_ATT_ROWS, C_ATT_ROWS), C_ATT_ROWS)
            for kvh in range(C_KV_HEADS):
                vblk = kv_ref[rows, C_KVW + kvh * C_HEAD_DIM:C_KVW + (kvh + 1) * C_HEAD_DIM]
                vtw_ref[kvh, i, 0:C_HEAD_DIM, :] = vblk.astype(F32).T.astype(BF16)
                vtw_ref[kvh, i, C_HEAD_DIM:d_aug, :] = ones_row(C_ATT_ROWS)
            return carry
        lax.fori_loop(0, seq // C_ATT_ROWS, fill_wide, 0)

    q_rows = pl.ds(pl.multiple_of(qb * blk, blk), blk)
    ext_t = ex_ref[q_rows, :].T
    idx_scale = (C_IDX_DIM ** -0.5) * (C_IDX_HEADS ** -0.5)
    w_idx = [ext_t[C_IDX_DIM + h:C_IDX_DIM + h + 1, :] * idx_scale for h in range(C_IDX_HEADS)]
    for p in range(n_pair):
        iqa_ref[p * blk:(p + 1) * blk, :] = iq_ref[:, p * LANES:(p + 1) * LANES]
    for kvh in range(C_KV_HEADS):
        for g in range(C_GROUP):
            h = kvh * C_GROUP + g
            qg_ref[kvh, g * blk:(g + 1) * blk, :] = q_ref[:, h * C_HEAD_DIM:(h + 1) * C_HEAD_DIM]

    s_iu = lax.broadcasted_iota(jnp.int32, (C_IDX_ROWS, blk), 0)
    t_iu = lax.broadcasted_iota(jnp.int32, (C_IDX_ROWS, blk), 1)
    nt = (((1,), (1,)), ((), ()))

    def idx_body(u, carry):
        mx, mn = carry
        rows = pl.ds(pl.multiple_of(u * C_IDX_ROWS, C_IDX_ROWS), C_IDX_ROWS)
        ik_lo = iklo_ref[rows, :]
        ik_hi = ikhi_ref[rows, :]
        acc = jnp.zeros((C_IDX_ROWS, blk), F32)
        for c in range(n_pair // 2):
            w = iqa_ref[c * 2 * blk:(c + 1) * 2 * blk, :]
            sc_even = lax.dot_general(ik_lo, w, nt, preferred_element_type=F32)
            sc_odd = lax.dot_general(ik_hi, w, nt, preferred_element_type=F32)
            for pp in range(2):
                p = 2 * c + pp
                sl = slice(pp * blk, (pp + 1) * blk)
                acc = acc + jnp.maximum(sc_even[:, sl], 0.0) * w_idx[2 * p]
                acc = acc + jnp.maximum(sc_odd[:, sl], 0.0) * w_idx[2 * p + 1]
        causal = (u * C_IDX_ROWS + s_iu) <= (qb * blk + t_iu)
        keys_ref[rows, :] = jnp.where(causal, acc, -jnp.inf)
        mx = jnp.maximum(mx, jnp.max(jnp.where(causal, acc, -jnp.inf), axis=0, keepdims=True))
        mn = jnp.minimum(mn, jnp.min(jnp.where(causal, acc, jnp.inf), axis=0, keepdims=True))
        return mx, mn

    mx, mn = lax.fori_loop(0, n_sel * (C_SEL_ROWS // C_IDX_ROWS), idx_body,
                           (jnp.full((1, blk), -jnp.inf, F32), jnp.full((1, blk), jnp.inf, F32)))

    def sel_rows(i):
        return pl.ds(pl.multiple_of(i * C_SEL_ROWS, C_SEL_ROWS), C_SEL_ROWS)

    def tree_sum(x):
        parts = [x[i] for i in range(x.shape[0])]
        while len(parts) > 1:
            parts = [parts[i] + parts[i + 1] for i in range(0, len(parts), 2)]
        return parts[0]

    n_lanes_acc = 8

    def count(pred):
        def body(i, cs):
            kb = keys_ref[sel_rows(i), :].reshape(sel_tiles, SUBLANES, blk)
            hit = pred(kb, i)
            cs = list(cs)
            for tile in range(sel_tiles):
                a = tile % n_lanes_acc
                cs[a] = jnp.where(hit[tile], cs[a] + 1.0, cs[a])
            return tuple(cs)
        cs = lax.fori_loop(0, n_sel, body, tuple(jnp.zeros((SUBLANES, blk), F32) for _ in range(n_lanes_acc)))
        return jnp.sum(tree_sum(jnp.stack(cs)), axis=0, keepdims=True)

    n_causal = (qb * blk + t_row + 1).astype(F32)
    k_eff = jnp.minimum(n_causal, float(ksel))
    lo0 = mn
    hi0 = mx + (jnp.abs(mx) * 1e-6 + 1e-30)
    cnt0 = n_causal

    def active_of(lo, hi, cnt):
        mid = lo + 0.5 * (hi - lo)
        return mid, (cnt > k_eff) & (mid > lo) & (mid < hi)

    def bis_cond(st):
        return st[3] > 0.0

    def bis_step(_, st):
        lo, hi, cnt = st
        mid, active = active_of(lo, hi, cnt)
        c = count(lambda kb, i: kb >= mid)
        ge = c >= k_eff
        return (jnp.where(active & ge, mid, lo), jnp.where(active & jnp.logical_not(ge), mid, hi),
                jnp.where(active & ge, c, cnt))

    def n_active(st):
        _, active = active_of(*st)
        return jnp.sum(jnp.where(active, 1.0, 0.0))

    def bis_body(st):
        st = lax.fori_loop(0, 2, bis_step, st[:3])
        return (*st, n_active(st))

    n_warm = jnp.where(qb * blk >= ksel, C_BISECT_WARMUP, 0)
    st = lax.fori_loop(0, n_warm, bis_step, (lo0, hi0, cnt0))
    thr, _, cnt, _ = lax.while_loop(bis_cond, bis_body, (*st, n_active(st)))

    def mask_body(i, carry):
        mask_ref[sel_rows(i), :] = jnp.where(keys_ref[sel_rows(i), :] >= thr, 0.0, MASK_NEG)
        return carry
    lax.fori_loop(0, n_sel, mask_body, 0)

    tie = cnt > k_eff

    @pl.when(jnp.sum(jnp.where(tie, 1.0, 0.0)) > 0.0)
    def _():
        tile_i = lax.broadcasted_iota(jnp.int32, (sel_tiles, SUBLANES, blk), 0)
        sub_i = lax.broadcasted_iota(jnp.int32, (sel_tiles, SUBLANES, blk), 1)
        s_i2 = lax.broadcasted_iota(jnp.int32, (C_SEL_ROWS, blk), 0)

        def pos3(i):
            return (i * C_SEL_ROWS + tile_i * SUBLANES + sub_i).astype(F32)

        need = k_eff - count(lambda kb, i: kb > thr)
        lo_i = jnp.full((1, blk), -1.0, F32)
        hi_i = jnp.zeros((1, blk), F32) + (n_sel * C_SEL_ROWS - 1).astype(F32)
        n_steps = int(math.ceil(math.log2(seq))) + 1

        def tb(_, st):
            lo_i, hi_i = st
            mid = jnp.floor((lo_i + hi_i) * 0.5)
            c = count(lambda kb, i: (kb == thr) & (pos3(i) <= mid))
            ge = c >= need
            return jnp.where(ge, lo_i, mid), jnp.where(ge, mid, hi_i)
        _, cut = lax.fori_loop(0, n_steps, tb, (lo_i, hi_i))

        def fix_body(i, carry):
            kb = keys_ref[sel_rows(i), :]
            pos = (i * C_SEL_ROWS + s_i2).astype(F32)
            sel = (kb > thr) | ((kb == thr) & (pos <= cut))
            mask_ref[sel_rows(i), :] = jnp.where(tie, jnp.where(sel, 0.0, MASK_NEG), mask_ref[sel_rows(i), :])
            return carry
        lax.fori_loop(0, n_sel, fix_body, 0)

    log2e = math.log2(math.e)
    n_far = jnp.maximum(qb - 1, 0)
    n_far_wide = lax.div(n_far, C_ATT_ROWS // blk)

    def qk_logits(rows):
        return [lax.dot_general(kv_ref[rows, kvh * C_HEAD_DIM:(kvh + 1) * C_HEAD_DIM], qg_ref[kvh], nt,
                                preferred_element_type=F32) for kvh in range(C_KV_HEADS)]

    def far_bias(h):
        return rb_ref[REL_BUCKETS - 1, h] * log2e

    def attend_online(row0, n_rows, near_bias, v_tile):
        rows = pl.ds(pl.multiple_of(row0, blk), n_rows)
        madd = mask_ref[rows, :]
        logits = qk_logits(rows)
        results = []
        for kvh in range(C_KV_HEADS):
            lg = logits[kvh]
            m_old = m_ref[kvh]
            ys, m_parts, shifts = [], [], []
            for g in range(C_GROUP):
                h = kvh * C_GROUP + g
                sl = slice(g * blk, (g + 1) * blk)
                if near_bias is None:
                    y = lg[:, sl] + madd
                    m_g = jnp.maximum(m_old[:, sl], jnp.max(y, axis=0, keepdims=True) + far_bias(h))
                    shifts.append(m_g - far_bias(h))
                else:
                    y = lg[:, sl] + (madd + near_bias(h) * log2e)
                    m_g = jnp.maximum(m_old[:, sl], jnp.max(y, axis=0, keepdims=True))
                    shifts.append(m_g)
                ys.append(y)
                m_parts.append(m_g)
            m_new = jnp.concatenate(m_parts, axis=1)
            alpha = jnp.exp2(m_old - m_new)
            p = jnp.concatenate([jnp.exp2(ys[g] - shifts[g]) for g in range(C_GROUP)], axis=1)
            pv = jnp.dot(v_tile(kvh), p.astype(BF16), preferred_element_type=F32)
            results.append((m_new, acc_ref[kvh] * alpha + pv))
        for kvh, (m_new, acc_new) in enumerate(results):
            m_ref[kvh] = m_new
            acc_ref[kvh] = acc_new

    def attend_direct(row0, n_rows, near_bias, v_tile):
        rows = pl.ds(pl.multiple_of(row0, blk), n_rows)
        madd = mask_ref[rows, :]
        logits = qk_logits(rows)
        results = []
        for kvh in range(C_KV_HEADS):
            lg = logits[kvh]
            parts = []
            for g in range(C_GROUP):
                h = kvh * C_GROUP + g
                sl = slice(g * blk, (g + 1) * blk)
                if near_bias is None:
                    parts.append(jnp.exp2(lg[:, sl] + madd))
                else:
                    parts.append(jnp.exp2(lg[:, sl] + (madd + near_bias(h) * log2e)))
            p = jnp.concatenate(parts, axis=1).astype(BF16)
            results.append(acc_ref[kvh] + jnp.dot(v_tile(kvh), p, preferred_element_type=F32))
        for kvh, acc_new in enumerate(results):
            acc_ref[kvh] = acc_new

    def apply_far_bias():
        for kvh in range(C_KV_HEADS):
            row = jnp.concatenate([jnp.exp2(jnp.full((1, blk), far_bias(kvh * C_GROUP + g), F32))
                                   for g in range(C_GROUP)], axis=1)
            acc_ref[kvh] = acc_ref[kvh] * row

    def run_attention(attend, after_far):
        acc_ref[...] = jnp.zeros_like(acc_ref)

        def one_block(j):
            return lambda kvh: vt_ref[kvh, j]

        def two_blocks(j):
            return lambda kvh: jnp.concatenate([vt_ref[kvh, j], vt_ref[kvh, j + 1]], axis=1)

        def far_wide_body(i, carry):
            attend(i * C_ATT_ROWS, C_ATT_ROWS, None, lambda kvh: vtw_ref[kvh, i])
            return carry
        lax.fori_loop(0, n_far_wide, far_wide_body, 0)
        j_left = n_far_wide * (C_ATT_ROWS // blk)
        n_left = n_far - j_left

        @pl.when(n_left >= 2)
        def _():
            attend(j_left * blk, 2 * blk, None, two_blocks(j_left))

        @pl.when(jnp.bitwise_and(n_left, 1) == 1)
        def _():
            j = j_left + n_left - 1
            attend(j * blk, blk, None, one_block(j))
        after_far()

        @pl.when(qb >= 1)
        def _():
            attend((qb - 1) * blk, 2 * blk,
                   lambda h: jnp.concatenate([bias_ref[h, 1], bias_ref[h, 0]], axis=0), two_blocks(qb - 1))

        @pl.when(qb == 0)
        def _():
            attend(0, blk, lambda h: bias_ref[h, 0], one_block(0))

    q_all = q_ref[...].astype(F32)
    q_sq = jnp.dot((q_all * q_all).astype(BF16), hsel_ref[...], preferred_element_type=F32)
    bound = jnp.sqrt(q_sq) * bvec_ref[0:1, :] + bvec_ref[1:2, :]
    direct_ok = jnp.max(bound) * 1.05 < C_DIRECT_EXP_LIMIT

    @pl.when(direct_ok)
    def _():
        run_attention(attend_direct, apply_far_bias)

    @pl.when(jnp.logical_not(direct_ok))
    def _():
        m_ref[...] = jnp.full(m_ref.shape, MASK_NEG, F32)
        run_attention(attend_online, lambda: None)

    for kvh in range(C_KV_HEADS):
        o_t = acc_ref[kvh, 0:C_HEAD_DIM, :] / acc_ref[kvh, C_HEAD_DIM:C_HEAD_DIM + 1, :]
        for g in range(C_GROUP):
            h = kvh * C_GROUP + g
            o_ref[:, h * C_HEAD_DIM:(h + 1) * C_HEAD_DIM] = o_t[:, g * blk:(g + 1) * blk].T.astype(o_ref.dtype)


def _dsa(p_main, p_extra, bias_tiles, rel_bias, bsz, seq):
    assert seq % C_SEL_ROWS == 0 and seq % C_ATT_ROWS == 0
    n = p_main.shape[0]
    nb = seq // C_QBLOCK
    blk = C_QBLOCK
    ksel = min(C_TOPK_MAX, seq // 4)
    n_pair = C_IDX_HEADS // 2
    d_aug = C_HEAD_DIM + C_VPAD
    head_sel = jnp.repeat(jnp.eye(C_HEADS, LANES, dtype=BF16), C_HEAD_DIM, axis=0)
    row = lambda b, i: b * nb + i
    return pl.pallas_call(
        functools.partial(_dsa_kernel, seq=seq, ksel=ksel),
        grid=(bsz, nb),
        in_specs=[
            pl.BlockSpec(memory_space=pltpu.SMEM),
            pl.BlockSpec((blk, C_WIDTH), lambda b, i: (row(b, i), 0)),
            pl.BlockSpec((seq, 2 * C_KVW), lambda b, i: (b, C_WIDTH // (2 * C_KVW))),
            pl.BlockSpec((blk, C_IDX_HEADS * C_IDX_DIM),
                         lambda b, i: (row(b, i), (C_WIDTH + 2 * C_KVW) // (C_IDX_HEADS * C_IDX_DIM))),
            pl.BlockSpec((seq, LANES), lambda b, i: (b, 0)),
            pl.BlockSpec((C_HEADS, 2, blk, blk), lambda b, i: (0, 0, 0, 0)),
            pl.BlockSpec((C_WIDTH, LANES), lambda b, i: (0, 0)),
        ],
        out_specs=pl.BlockSpec((blk, C_WIDTH), lambda b, i: (row(b, i), 0)),
        out_shape=jax.ShapeDtypeStruct((n, C_WIDTH), BF16),
        scratch_shapes=[
            pltpu.VMEM((seq, blk), F32),
            pltpu.VMEM((seq, blk), F32),
            pltpu.VMEM((C_KV_HEADS, nb, d_aug, blk), BF16),
            pltpu.VMEM((C_KV_HEADS, seq // C_ATT_ROWS, d_aug, C_ATT_ROWS), BF16),
            pltpu.VMEM((seq, LANES), BF16),
            pltpu.VMEM((seq, LANES), BF16),
            pltpu.VMEM((n_pair * blk, LANES), BF16),
            pltpu.VMEM((C_KV_HEADS, C_GROUP * blk, C_HEAD_DIM), BF16),
            pltpu.VMEM((C_KV_HEADS, d_aug, C_GROUP * blk), F32),
            pltpu.VMEM((C_KV_HEADS, 1, C_GROUP * blk), F32),
            pltpu.VMEM((SUBLANES, LANES), F32),
        ],
        compiler_params=_params("parallel", "arbitrary"),
        name="dsa",
    )(rel_bias, p_main, p_main, p_main, p_extra, bias_tiles, head_sel)


AB_MAIN = 2 * A_WIDTH + 2 * B_HEADS * B_DK + 2 * B_WIDTH
C_MAIN = C_WIDTH + 2 * C_KVW + C_IDX_HEADS * C_IDX_DIM


def _extra_cols(w_in_all, n_main):
    n_extra = w_in_all.shape[2] - n_main
    return jnp.pad(w_in_all[:, :, n_main:], ((0, 0), (0, 0), (0, LANES - n_extra))).astype(BF16)


def _even_mixer(h, g, w_in, w_extra, li, v_ln_g, w_s, b_s, gate_w2, gate_b, out_norm_g, w_out, bsz, seq):
    p_main, p_extra = _norm_matmul(h, g, w_in, w_extra, li, AB_MAIN, F32)
    a_out = _sgu(p_main, v_ln_g, w_s, b_s.T)
    gate_w2_pad = jnp.pad(gate_w2, ((0, LANES - B_RANK), (0, 0))).astype(BF16)
    b_out = _gla(p_main, p_extra, gate_w2_pad, gate_b, out_norm_g, bsz, seq)
    return _proj_residual(h, [a_out, b_out], w_out, li)


def _odd_mixer(h, g, w_in, w_extra, li, w_out, bias_tiles, rel_bias, bsz, seq):
    p_main, p_extra = _norm_matmul(h, g, w_in, w_extra, li, C_MAIN, BF16)
    o = _dsa(p_main, p_extra, bias_tiles, rel_bias, bsz, seq)
    return _proj_residual(h, [o], w_out, li)


def _scaled_c_in(c_w_in):
    q_scale = (C_HEAD_DIM ** -0.5) * math.log2(math.e)
    col_scale = jnp.where(jnp.arange(c_w_in.shape[2]) < C_WIDTH, q_scale, 1.0).astype(F32)
    return (c_w_in * col_scale).astype(BF16)


def kernel(x, norm_mix_g, norm_ffn_g, final_norm_g, ab_w_in, a_v_ln_g, a_w_s, a_b_s, b_gate_w2,
           b_gate_b, b_out_norm_g, ab_w_out, c_w_in, c_w_out, rel_bias, ffn_w1, ffn_w2):
    bsz, seq, d = x.shape
    depth = norm_mix_g.shape[0]
    h = x.reshape(bsz * seq, d)
    bias_tiles = _bias_tiles(rel_bias)
    ab_in, ab_in_extra, ab_out = ab_w_in.astype(BF16), _extra_cols(ab_w_in, AB_MAIN), ab_w_out.astype(BF16)
    c_in, c_in_extra, c_out = _scaled_c_in(c_w_in), _extra_cols(c_w_in, C_MAIN), c_w_out.astype(BF16)
    w1, w2 = ffn_w1.astype(BF16), ffn_w2.astype(BF16)
    for layer in range(depth):
        i = layer // 2
        if layer % 2 == 0:
            h = _even_mixer(h, norm_mix_g[layer], ab_in, ab_in_extra, i, a_v_ln_g[i], a_w_s[i], a_b_s[i],
                            b_gate_w2[i], b_gate_b[i], b_out_norm_g[i], ab_out, bsz, seq)
        else:
            h = _odd_mixer(h, norm_mix_g[layer], c_in, c_in_extra, i, c_out, bias_tiles, rel_bias, bsz, seq)
        h = _ffn(h, norm_ffn_g[layer], w1, w2, layer, final_norm_g, layer == depth - 1)
    return h.reshape(bsz, seq, d)
```

```python
import functools
import math

import jax
import jax.numpy as jnp
from jax import lax
from jax.experimental import pallas as pl
from jax.experimental.pallas import tpu as pltpu

F32 = jnp.float32
BF16 = jnp.bfloat16
EPS = 1e-6

VMEM_LIMIT_BYTES = 56 * 1024 * 1024
LANES = 128
SUBLANES = 8

A_GROUPS = 8
A_DIM = 128
A_CHUNK = 128
A_WIDTH = A_GROUPS * A_DIM
B_HEADS = 4
B_DK = 128
B_DV = 256
B_RANK = 16
B_TAU = 16.0
B_CHUNK = 64
B_WIDTH = B_HEADS * B_DV
C_HEADS = 16
C_KV_HEADS = 4
C_GROUP = C_HEADS // C_KV_HEADS
C_HEAD_DIM = 128
C_IDX_HEADS = 16
C_IDX_DIM = 64
C_TOPK_MAX = 256
C_QBLOCK = 128
C_WIDTH = C_HEADS * C_HEAD_DIM
C_KVW = C_KV_HEADS * C_HEAD_DIM
C_SEL_ROWS = 4 * C_QBLOCK
C_IDX_ROWS = 4 * C_QBLOCK
C_ATT_ROWS = 4 * C_QBLOCK
C_VPAD = 16
C_BISECT_WARMUP = 16
C_DIRECT_EXP_LIMIT = 60.0
REL_BUCKETS = 32
REL_MAX_DIST = 128
MASK_NEG = -1e30


def _params(*sem):
    return pltpu.CompilerParams(dimension_semantics=sem, vmem_limit_bytes=VMEM_LIMIT_BYTES)


def _rms_rows(x, g):
    ms = jnp.mean(x * x, axis=-1, keepdims=True)
    return x * lax.rsqrt(ms + EPS) * g


def _norm_matmul_kernel(x_ref, g_ref, w_ref, we_ref, o_ref, oe_ref, xn_ref):
    @pl.when(pl.program_id(1) == 0)
    def _():
        xn = _rms_rows(x_ref[...], g_ref[...]).astype(BF16)
        xn_ref[...] = xn
        oe_ref[...] = jnp.dot(xn, we_ref[...], preferred_element_type=F32)

    o_ref[...] = jnp.dot(xn_ref[...], w_ref[...], preferred_element_type=F32).astype(o_ref.dtype)


def _norm_matmul(x, g, w_all, w_extra, li, nm, out_dtype, tm=1024, tn=1024):
    n, d = x.shape
    assert nm % tn == 0 and nm <= w_all.shape[2]
    return pl.pallas_call(
        _norm_matmul_kernel,
        grid=(n // tm, nm // tn),
        in_specs=[
            pl.BlockSpec((tm, d), lambda i, j: (i, 0)),
            pl.BlockSpec((1, d), lambda i, j: (0, 0)),
            pl.BlockSpec((None, d, tn), lambda i, j: (li, 0, j)),
            pl.BlockSpec((None, d, LANES), lambda i, j: (li, 0, 0)),
        ],
        out_specs=[
            pl.BlockSpec((tm, tn), lambda i, j: (i, j)),
            pl.BlockSpec((tm, LANES), lambda i, j: (i, 0)),
        ],
        out_shape=[jax.ShapeDtypeStruct((n, nm), out_dtype), jax.ShapeDtypeStruct((n, LANES), F32)],
        scratch_shapes=[pltpu.VMEM((tm, d), BF16)],
        compiler_params=_params("parallel", "arbitrary"),
        name="norm_matmul",
    )(x, g.reshape(1, d), w_all, w_extra)


def _proj_res_kernel(*refs, n_in):
    h_ref = refs[0]
    o_ref = refs[1 + 2 * n_in]
    acc = h_ref[...]
    for k in range(n_in):
        acc = acc + jnp.dot(refs[1 + k][...], refs[1 + n_in + k][...], preferred_element_type=F32)
    o_ref[...] = acc


def _proj_residual(h, xs, w, li, tm=512, tn=2048):
    n, d = h.shape
    n_in = len(xs)
    kx = xs[0].shape[1]
    assert all(x.shape[1] == kx for x in xs) and kx * n_in == w.shape[1]
    ws = [w] * n_in
    in_specs = [pl.BlockSpec((tm, tn), lambda i, j: (i, j))]
    in_specs += [pl.BlockSpec((tm, kx), lambda i, j: (i, 0)) for _ in xs]
    in_specs += [pl.BlockSpec((None, kx, tn), functools.partial(lambda i, j, k: (li, k, j), k=k))
                 for k in range(n_in)]
    return pl.pallas_call(
        functools.partial(_proj_res_kernel, n_in=n_in),
        grid=(n // tm, d // tn),
        in_specs=in_specs,
        out_specs=pl.BlockSpec((tm, tn), lambda i, j: (i, j)),
        out_shape=jax.ShapeDtypeStruct((n, d), F32),
        compiler_params=_params("parallel", "parallel"),
        name="proj_residual",
    )(h, *xs, *ws)


def _ffn_kernel(x_ref, g_ref, w1_ref, w2_ref, fg_ref, o_ref, xn_ref, acc_ref, *, final_norm):
    f = pl.program_id(1)

    @pl.when(f == 0)
    def _():
        xn_ref[...] = _rms_rows(x_ref[...], g_ref[...]).astype(BF16)
        acc_ref[...] = jnp.zeros_like(acc_ref)

    h1 = jnp.dot(xn_ref[...], w1_ref[...], preferred_element_type=F32)
    h1 = jnp.square(jnp.maximum(h1, 0.0)).astype(BF16)
    acc_ref[...] += jnp.dot(h1, w2_ref[...], preferred_element_type=F32)

    @pl.when(f == pl.num_programs(1) - 1)
    def _():
        y = x_ref[...] + acc_ref[...]
        if final_norm:
            y = _rms_rows(y, fg_ref[...])
        o_ref[...] = y


def _ffn(x, g, w1, w2, layer, final_g, final_norm, tm=512, tf=1024):
    n, d = x.shape
    dff = w1.shape[2]
    return pl.pallas_call(
        functools.partial(_ffn_kernel, final_norm=final_norm),
        grid=(n // tm, dff // tf),
        in_specs=[
            pl.BlockSpec((tm, d), lambda i, f: (i, 0)),
            pl.BlockSpec((1, d), lambda i, f: (0, 0)),
            pl.BlockSpec((None, d, tf), lambda i, f: (layer, 0, f)),
            pl.BlockSpec((None, tf, d), lambda i, f: (layer, f, 0)),
            pl.BlockSpec((1, d), lambda i, f: (0, 0)),
        ],
        out_specs=pl.BlockSpec((tm, d), lambda i, f: (i, 0)),
        out_shape=jax.ShapeDtypeStruct((n, d), F32),
        scratch_shapes=[pltpu.VMEM((tm, d), BF16), pltpu.VMEM((tm, d), F32)],
        compiler_params=_params("parallel", "arbitrary"),
        name="ffn",
    )(x, g.reshape(1, d), w1, w2, final_g.reshape(1, d))


def _gelu(x):
    return 0.5 * x * (1.0 + lax.erf(x * math.sqrt(0.5)))


def _sgu_kernel(u_ref, v_ref, lng_ref, w_ref, bt_ref, o_ref):
    t_i = lax.broadcasted_iota(jnp.int32, (A_CHUNK, A_CHUNK), 0)
    s_i = lax.broadcasted_iota(jnp.int32, (A_CHUNK, A_CHUNK), 1)
    causal = s_i <= t_i
    n_sub = u_ref.shape[0] // A_CHUNK
    for g in range(A_GROUPS):
        sl = slice(g * A_DIM, (g + 1) * A_DIM)
        w = jnp.where(causal, w_ref[g], 0.0).astype(BF16)
        for sub in range(n_sub):
            rsl = slice(sub * A_CHUNK, (sub + 1) * A_CHUNK)
            v = _gelu(v_ref[rsl, sl])
            mu = jnp.mean(v, axis=-1, keepdims=True)
            vc = v - mu
            var = jnp.mean(vc * vc, axis=-1, keepdims=True)
            vn = vc * lax.rsqrt(var + EPS) * lng_ref[g:g + 1, :]
            z = jnp.dot(w, vn.astype(BF16), preferred_element_type=F32) + bt_ref[:, g:g + 1]
            o_ref[rsl, sl] = (_gelu(u_ref[rsl, sl]) * z).astype(o_ref.dtype)


def _sgu(p_main, ln_g, w_s, b_s_t, chunks_per_step=8):
    n = p_main.shape[0]
    rows = A_CHUNK * chunks_per_step
    return pl.pallas_call(
        _sgu_kernel,
        grid=(n // rows,),
        in_specs=[
            pl.BlockSpec((rows, A_WIDTH), lambda i: (i, 0)),
            pl.BlockSpec((rows, A_WIDTH), lambda i: (i, 1)),
            pl.BlockSpec((A_GROUPS, A_DIM), lambda i: (0, 0)),
            pl.BlockSpec((A_GROUPS, A_CHUNK, A_CHUNK), lambda i: (0, 0, 0)),
            pl.BlockSpec((A_CHUNK, A_GROUPS), lambda i: (0, 0)),
        ],
        out_specs=pl.BlockSpec((rows, A_WIDTH), lambda i: (i, 0)),
        out_shape=jax.ShapeDtypeStruct((n, A_WIDTH), BF16),
        compiler_params=_params("parallel"),
        name="sgu",
    )(p_main, p_main, ln_g, w_s, b_s_t)


def _cumsum_rows(x):
    n = x.shape[0]
    row = lax.broadcasted_iota(jnp.int32, x.shape, 0)
    sh = 1
    while sh < n:
        x = x + jnp.where(row >= sh, pltpu.roll(x, sh, axis=0), 0.0)
        sh *= 2
    return x


def _log_sigmoid(x):
    return jnp.minimum(x, 0.0) - jnp.log1p(jnp.exp(-jnp.abs(x)))


def _gla_kernel(q_ref, k_ref, v_ref, r_ref, g_ref, w2_ref, gb_ref, ng_ref, o_ref, s_ref):
    @pl.when(pl.program_id(1) == 0)
    def _():
        s_ref[...] = jnp.zeros_like(s_ref)

    c = B_CHUNK
    n_seq = q_ref.shape[0]
    n_sub = q_ref.shape[1] // c
    i_i = lax.broadcasted_iota(jnp.int32, (c, c), 0)
    j_i = lax.broadcasted_iota(jnp.int32, (c, c), 1)
    causal = j_i <= i_i
    for b in range(n_seq):
        gate = jnp.dot(g_ref[b].astype(BF16), w2_ref[...], preferred_element_type=F32) + gb_ref[...]
        log_a = _log_sigmoid(gate) * (1.0 / B_TAU)
        for h in range(B_HEADS):
            ksl = slice(h * B_DK, (h + 1) * B_DK)
            vsl = slice(h * B_DV, (h + 1) * B_DV)
            state = s_ref[b, h]
            for sub in range(n_sub):
                rsl = slice(sub * c, (sub + 1) * c)
                cum = _cumsum_rows(log_a[rsl, ksl])
                last = cum[c - 1:c, :]
                ref = 0.5 * last
                q = q_ref[b, rsl, ksl] * (B_DK ** -0.5)
                k = k_ref[b, rsl, ksl]
                v = v_ref[b, rsl, vsl].astype(BF16)
                qe = (q * jnp.exp(cum - ref)).astype(BF16)
                ke = (k * jnp.exp(ref - cum)).astype(BF16)
                scores = lax.dot_general(qe, ke, (((1,), (1,)), ((), ())), preferred_element_type=F32)
                scores = jnp.where(causal, scores, 0.0).astype(BF16)
                o = jnp.dot(scores, v, preferred_element_type=F32)
                q_inter = (q * jnp.exp(cum)).astype(BF16)
                o = o + jnp.dot(q_inter, state.astype(BF16), preferred_element_type=F32)
                cum_t = cum.T
                last_t = cum_t[:, c - 1:c]
                k_state_t = (k.T * jnp.exp(last_t - cum_t)).astype(BF16)
                state = jnp.exp(last_t) * state + jnp.dot(k_state_t, v, preferred_element_type=F32)
                o = _rms_rows(o, ng_ref[...])
                r = r_ref[b, rsl, vsl]
                o_ref[b, rsl, vsl] = (o * (r * jax.nn.sigmoid(r))).astype(o_ref.dtype)
            s_ref[b, h] = state


def _gla(p_main, p_extra, gate_w2_pad, gate_b, norm_g, bsz, seq, chunks_per_step=4, seqs_per_step=2):
    n = p_main.shape[0]
    c = B_CHUNK * chunks_per_step
    qk_w = B_HEADS * B_DK
    ns = seqs_per_step
    p3 = p_main.reshape(bsz, seq, p_main.shape[1])
    e3 = p_extra.reshape(bsz, seq, LANES)
    out = pl.pallas_call(
        _gla_kernel,
        grid=(bsz // ns, seq // c),
        in_specs=[
            pl.BlockSpec((ns, c, qk_w), lambda b, i: (b, i, (2 * A_WIDTH) // qk_w)),
            pl.BlockSpec((ns, c, qk_w), lambda b, i: (b, i, (2 * A_WIDTH) // qk_w + 1)),
            pl.BlockSpec((ns, c, B_WIDTH), lambda b, i: (b, i, (2 * A_WIDTH + 2 * qk_w) // B_WIDTH)),
            pl.BlockSpec((ns, c, B_WIDTH), lambda b, i: (b, i, (2 * A_WIDTH + 2 * qk_w) // B_WIDTH + 1)),
            pl.BlockSpec((ns, c, LANES), lambda b, i: (b, i, 0)),
            pl.BlockSpec((LANES, qk_w), lambda b, i: (0, 0)),
            pl.BlockSpec((1, qk_w), lambda b, i: (0, 0)),
            pl.BlockSpec((1, B_DV), lambda b, i: (0, 0)),
        ],
        out_specs=pl.BlockSpec((ns, c, B_WIDTH), lambda b, i: (b, i, 0)),
        out_shape=jax.ShapeDtypeStruct((bsz, seq, B_WIDTH), BF16),
        scratch_shapes=[pltpu.VMEM((ns, B_HEADS, B_DK, B_DV), F32)],
        compiler_params=_params("parallel", "arbitrary"),
        name="gla",
    )(p3, p3, p3, p3, e3, gate_w2_pad, gate_b.reshape(1, qk_w), norm_g.reshape(1, B_DV))
    return out.reshape(n, B_WIDTH)


def _t5_bucket(dist):
    max_exact = REL_BUCKETS // 2
    d = jnp.maximum(dist, 1).astype(F32)
    large = max_exact + (jnp.log(d / max_exact) / math.log(REL_MAX_DIST / max_exact)
                         * (REL_BUCKETS - max_exact)).astype(jnp.int32)
    large = jnp.minimum(large, REL_BUCKETS - 1)
    return jnp.where(dist < max_exact, dist, large)


def _bias_kernel(rb_ref, o_ref):
    s_i = lax.broadcasted_iota(jnp.int32, (C_QBLOCK, C_QBLOCK), 0)
    t_i = lax.broadcasted_iota(jnp.int32, (C_QBLOCK, C_QBLOCK), 1)
    for delta in range(2):
        bucket = _t5_bucket(jnp.maximum(delta * C_QBLOCK + t_i - s_i, 0))
        for h in range(C_HEADS):
            acc = jnp.zeros((C_QBLOCK, C_QBLOCK), F32)
            for b in range(REL_BUCKETS):
                acc = jnp.where(bucket == b, rb_ref[b, h], acc)
            o_ref[h, delta] = acc


def _bias_tiles(rel_bias):
    return pl.pallas_call(
        _bias_kernel,
        in_specs=[pl.BlockSpec(memory_space=pltpu.SMEM)],
        out_specs=pl.BlockSpec(memory_space=pltpu.VMEM),
        out_shape=jax.ShapeDtypeStruct((C_HEADS, 2, C_QBLOCK, C_QBLOCK), F32),
        name="bias_tiles",
    )(rel_bias)


def _dsa_kernel(rb_ref, q_ref, kv_ref, iq_ref, ex_ref, bias_ref, hsel_ref, o_ref,
                keys_ref, mask_ref, vt_ref, vtw_ref, iklo_ref, ikhi_ref, iqa_ref, qg_ref,
                acc_ref, m_ref, bvec_ref, *, seq, ksel):
    qb = pl.program_id(1)
    blk = C_QBLOCK
    nb = seq // blk
    n_pair = C_IDX_HEADS // 2
    n_sel = lax.shift_right_logical(qb, 2) + 1
    sel_tiles = C_SEL_ROWS // SUBLANES
    lane = lax.broadcasted_iota(jnp.int32, (blk, LANES), 1)
    t_row = lax.broadcasted_iota(jnp.int32, (1, blk), 1)
    d_aug = C_HEAD_DIM + C_VPAD

    def ones_row(width):
        r = lax.broadcasted_iota(jnp.int32, (C_VPAD, width), 0)
        return jnp.where(r == 0, 1.0, 0.0).astype(BF16)

    @pl.when(qb == 0)
    def _():
        def fill(j, k_sq):
            rows = pl.ds(pl.multiple_of(j * blk, blk), blk)
            ex = ex_ref[rows, :]
            iklo_ref[rows, :] = jnp.where(lane < C_IDX_DIM, ex, 0.0).astype(BF16)
            ikhi_ref[rows, :] = jnp.where(lane >= C_IDX_DIM, pltpu.roll(ex, C_IDX_DIM, axis=1),
                                          0.0).astype(BF16)
            for kvh in range(C_KV_HEADS):
                vblk = kv_ref[rows, C_KVW + kvh * C_HEAD_DIM:C_KVW + (kvh + 1) * C_HEAD_DIM]
                vt_ref[kvh, j, 0:C_HEAD_DIM, :] = vblk.astype(F32).T.astype(BF16)
                vt_ref[kvh, j, C_HEAD_DIM:d_aug, :] = ones_row(blk)
            k_sq_new = []
            for kvh in range(C_KV_HEADS):
                kblk = kv_ref[rows, kvh * C_HEAD_DIM:(kvh + 1) * C_HEAD_DIM].astype(F32)
                k_sq_new.append(jnp.maximum(k_sq[kvh], jnp.sum(kblk * kblk, axis=1, keepdims=True)))
            return tuple(k_sq_new)
        k_sq = lax.fori_loop(0, nb, fill, tuple(jnp.zeros((blk, 1), F32) for _ in range(C_KV_HEADS)))
        lane_row = lax.broadcasted_iota(jnp.int32, (1, LANES), 1)
        kn_row = jnp.zeros((1, LANES), F32)
        b_row = jnp.zeros((1, LANES), F32)
        for kvh in range(C_KV_HEADS):
            kn_row = jnp.where(lax.div(lane_row, C_GROUP) == kvh, jnp.max(jnp.sqrt(k_sq[kvh])), kn_row)
        for h in range(C_HEADS):
            b_abs = jnp.abs(rb_ref[0, h])
            for b in range(1, REL_BUCKETS):
                b_abs = jnp.maximum(b_abs, jnp.abs(rb_ref[b, h]))
            b_row = jnp.where(lane_row == h, b_abs * math.log2(math.e), b_row)
        bvec_ref[0:1, :] = kn_row
        bvec_ref[1:2, :] = b_row

        def fill_wide(i, carry):
            rows = pl.ds(pl.multiple_of(i * C_ATT_ROWS, C_ATT_ROWS), C_ATT_ROWS)
            for kvh in range(C_KV_HEADS):
                vblk = kv_ref[rows, C_KVW + kvh * C_HEAD_DIM:C_KVW + (kvh + 1) * C_HEAD_DIM]
                vtw_ref[kvh, i, 0:C_HEAD_DIM, :] = vblk.astype(F32).T.astype(BF16)
                vtw_ref[kvh, i, C_HEAD_DIM:d_aug, :] = ones_row(C_ATT_ROWS)
            return carry
        lax.fori_loop(0, seq // C_ATT_ROWS, fill_wide, 0)

    q_rows = pl.ds(pl.multiple_of(qb * blk, blk), blk)
    ext_t = ex_ref[q_rows, :].T
    idx_scale = (C_IDX_DIM ** -0.5) * (C_IDX_HEADS ** -0.5)
    w_idx = [ext_t[C_IDX_DIM + h:C_IDX_DIM + h + 1, :] * idx_scale for h in range(C_IDX_HEADS)]
    for p in range(n_pair):
        iqa_ref[p * blk:(p + 1) * blk, :] = iq_ref[:, p * LANES:(p + 1) * LANES]
    for kvh in range(C_KV_HEADS):
        for g in range(C_GROUP):
            h = kvh * C_GROUP + g
            qg_ref[kvh, g * blk:(g + 1) * blk, :] = q_ref[:, h * C_HEAD_DIM:(h + 1) * C_HEAD_DIM]

    q_all = q_ref[...].astype(F32)
    q_sq = jnp.dot((q_all * q_all).astype(BF16), hsel_ref[...], preferred_element_type=F32)
    bound = jnp.sqrt(q_sq) * bvec_ref[0:1, :] + bvec_ref[1:2, :]
    direct_ok = jnp.max(bound) * 1.05 < C_DIRECT_EXP_LIMIT

    s_iu = lax.broadcasted_iota(jnp.int32, (C_IDX_ROWS, blk), 0)
    t_iu = lax.broadcasted_iota(jnp.int32, (C_IDX_ROWS, blk), 1)
    nt = (((1,), (1,)), ((), ()))

    def idx_body(u, carry):
        mx, mn = carry
        rows = pl.ds(pl.multiple_of(u * C_IDX_ROWS, C_IDX_ROWS), C_IDX_ROWS)
        ik_lo = iklo_ref[rows, :]
        ik_hi = ikhi_ref[rows, :]
        acc = jnp.zeros((C_IDX_ROWS, blk), F32)
        for c in range(n_pair // 2):
            w = iqa_ref[c * 2 * blk:(c + 1) * 2 * blk, :]
            sc_even = lax.dot_general(ik_lo, w, nt, preferred_element_type=F32)
            sc_odd = lax.dot_general(ik_hi, w, nt, preferred_element_type=F32)
            for pp in range(2):
                p = 2 * c + pp
                sl = slice(pp * blk, (pp + 1) * blk)
                acc = acc + jnp.maximum(sc_even[:, sl], 0.0) * w_idx[2 * p]
                acc = acc + jnp.maximum(sc_odd[:, sl], 0.0) * w_idx[2 * p + 1]
        causal = (u * C_IDX_ROWS + s_iu) <= (qb * blk + t_iu)
        keys_ref[rows, :] = jnp.where(causal, acc, -jnp.inf)
        mx = jnp.maximum(mx, jnp.max(jnp.where(causal, acc, -jnp.inf), axis=0, keepdims=True))
        mn = jnp.minimum(mn, jnp.min(jnp.where(causal, acc, jnp.inf), axis=0, keepdims=True))
        return mx, mn

    mx, mn = lax.fori_loop(0, n_sel * (C_SEL_ROWS // C_IDX_ROWS), idx_body,
                           (jnp.full((1, blk), -jnp.inf, F32), jnp.full((1, blk), jnp.inf, F32)))

    def sel_rows(i):
        return pl.ds(pl.multiple_of(i * C_SEL_ROWS, C_SEL_ROWS), C_SEL_ROWS)

    def tree_sum(x):
        parts = [x[i] for i in range(x.shape[0])]
        while len(parts) > 1:
            parts = [parts[i] + parts[i + 1] for i in range(0, len(parts), 2)]
        return parts[0]

    n_lanes_acc = 8

    def count(pred):
        def body(i, cs):
            kb = keys_ref[sel_rows(i), :].reshape(sel_tiles, SUBLANES, blk)
            hit = pred(kb, i)
            cs = list(cs)
            for tile in range(sel_tiles):
                a = tile % n_lanes_acc
                cs[a] = jnp.where(hit[tile], cs[a] + 1.0, cs[a])
            return tuple(cs)
        cs = lax.fori_loop(0, n_sel, body, tuple(jnp.zeros((SUBLANES, blk), F32) for _ in range(n_lanes_acc)))
        return jnp.sum(tree_sum(jnp.stack(cs)), axis=0, keepdims=True)

    n_causal = (qb * blk + t_row + 1).astype(F32)
    k_eff = jnp.minimum(n_causal, float(ksel))
    lo0 = mn
    hi0 = mx + (jnp.abs(mx) * 1e-6 + 1e-30)
    cnt0 = n_causal

    def active_of(lo, hi, cnt):
        mid = lo + 0.5 * (hi - lo)
        return mid, (cnt > k_eff) & (mid > lo) & (mid < hi)

    def bis_cond(st):
        return st[3] > 0.0

    def bis_step(_, st):
        lo, hi, cnt = st
        mid, active = active_of(lo, hi, cnt)
        c = count(lambda kb, i: kb >= mid)
        ge = c >= k_eff
        return (jnp.where(active & ge, mid, lo), jnp.where(active & jnp.logical_not(ge), mid, hi),
                jnp.where(active & ge, c, cnt))

    def n_active(st):
        _, active = active_of(*st)
        return jnp.sum(jnp.where(active, 1.0, 0.0))

    def bis_body(st):
        st = lax.fori_loop(0, 2, bis_step, st[:3])
        return (*st, n_active(st))

    n_warm = jnp.where(qb * blk >= ksel, C_BISECT_WARMUP, 0)
    st = lax.fori_loop(0, n_warm, bis_step, (lo0, hi0, cnt0))
    thr, _, cnt, _ = lax.while_loop(bis_cond, bis_body, (*st, n_active(st)))

    def mask_body(i, carry):
        mask_ref[sel_rows(i), :] = jnp.where(keys_ref[sel_rows(i), :] >= thr, 0.0, MASK_NEG)
        return carry
    lax.fori_loop(0, n_sel, mask_body, 0)

    tie = cnt > k_eff

    @pl.when(jnp.sum(jnp.where(tie, 1.0, 0.0)) > 0.0)
    def _():
        tile_i = lax.broadcasted_iota(jnp.int32, (sel_tiles, SUBLANES, blk), 0)
        sub_i = lax.broadcasted_iota(jnp.int32, (sel_tiles, SUBLANES, blk), 1)
        s_i2 = lax.broadcasted_iota(jnp.int32, (C_SEL_ROWS, blk), 0)

        def pos3(i):
            return (i * C_SEL_ROWS + tile_i * SUBLANES + sub_i).astype(F32)

        need = k_eff - count(lambda kb, i: kb > thr)
        lo_i = jnp.full((1, blk), -1.0, F32)
        hi_i = jnp.zeros((1, blk), F32) + (n_sel * C_SEL_ROWS - 1).astype(F32)
        n_steps = int(math.ceil(math.log2(seq))) + 1

        def tb(_, st):
            lo_i, hi_i = st
            mid = jnp.floor((lo_i + hi_i) * 0.5)
            c = count(lambda kb, i: (kb == thr) & (pos3(i) <= mid))
            ge = c >= need
            return jnp.where(ge, lo_i, mid), jnp.where(ge, mid, hi_i)
        _, cut = lax.fori_loop(0, n_steps, tb, (lo_i, hi_i))

        def fix_body(i, carry):
            kb = keys_ref[sel_rows(i), :]
            pos = (i * C_SEL_ROWS + s_i2).astype(F32)
            sel = (kb > thr) | ((kb == thr) & (pos <= cut))
            mask_ref[sel_rows(i), :] = jnp.where(tie, jnp.where(sel, 0.0, MASK_NEG), mask_ref[sel_rows(i), :])
            return carry
        lax.fori_loop(0, n_sel, fix_body, 0)

    log2e = math.log2(math.e)
    n_far = jnp.maximum(qb - 1, 0)
    n_far_wide = lax.div(n_far, C_ATT_ROWS // blk)

    def qk_logits(rows):
        return [lax.dot_general(kv_ref[rows, kvh * C_HEAD_DIM:(kvh + 1) * C_HEAD_DIM], qg_ref[kvh], nt,
                                preferred_element_type=F32) for kvh in range(C_KV_HEADS)]

    def far_bias(h):
        return rb_ref[REL_BUCKETS - 1, h] * log2e

    def attend_online(row0, n_rows, near_bias, v_tile):
        rows = pl.ds(pl.multiple_of(row0, blk), n_rows)
        madd = mask_ref[rows, :]
        logits = qk_logits(rows)
        results = []
        for kvh in range(C_KV_HEADS):
            lg = logits[kvh]
            m_old = m_ref[kvh]
            ys, m_parts, shifts = [], [], []
            for g in range(C_GROUP):
                h = kvh * C_GROUP + g
                sl = slice(g * blk, (g + 1) * blk)
                if near_bias is None:
                    y = lg[:, sl] + madd
                    m_g = jnp.maximum(m_old[:, sl], jnp.max(y, axis=0, keepdims=True) + far_bias(h))
                    shifts.append(m_g - far_bias(h))
                else:
                    y = lg[:, sl] + (madd + near_bias(h) * log2e)
                    m_g = jnp.maximum(m_old[:, sl], jnp.max(y, axis=0, keepdims=True))
                    shifts.append(m_g)
                ys.append(y)
                m_parts.append(m_g)
            m_new = jnp.concatenate(m_parts, axis=1)
            alpha = jnp.exp2(m_old - m_new)
            p = jnp.concatenate([jnp.exp2(ys[g] - shifts[g]) for g in range(C_GROUP)], axis=1)
            pv = jnp.dot(v_tile(kvh), p.astype(BF16), preferred_element_type=F32)
            results.append((m_new, acc_ref[kvh] * alpha + pv))
        for kvh, (m_new, acc_new) in enumerate(results):
            m_ref[kvh] = m_new
            acc_ref[kvh] = acc_new

    def attend_direct(row0, n_rows, near_bias, v_tile):
        rows = pl.ds(pl.multiple_of(row0, blk), n_rows)
        madd = mask_ref[rows, :]
        logits = qk_logits(rows)
        results = []
        for kvh in range(C_KV_HEADS):
            lg = logits[kvh]
            parts = []
            for g in range(C_GROUP):
                h = kvh * C_GROUP + g
                sl = slice(g * blk, (g + 1) * blk)
                if near_bias is None:
                    parts.append(jnp.exp2(lg[:, sl] + madd))
                else:
                    parts.append(jnp.exp2(lg[:, sl] + (madd + near_bias(h) * log2e)))
            p = jnp.concatenate(parts, axis=1).astype(BF16)
            results.append(acc_ref[kvh] + jnp.dot(v_tile(kvh), p, preferred_element_type=F32))
        for kvh, acc_new in enumerate(results):
            acc_ref[kvh] = acc_new

    def apply_far_bias():
        for kvh in range(C_KV_HEADS):
            row = jnp.concatenate([jnp.exp2(jnp.full((1, blk), far_bias(kvh * C_GROUP + g), F32))
                                   for g in range(C_GROUP)], axis=1)
            acc_ref[kvh] = acc_ref[kvh] * row

    def run_attention(attend, after_far):
        acc_ref[...] = jnp.zeros_like(acc_ref)

        def one_block(j):
            return lambda kvh: vt_ref[kvh, j]

        def two_blocks(j):
            return lambda kvh: jnp.concatenate([vt_ref[kvh, j], vt_ref[kvh, j + 1]], axis=1)

        def far_wide_body(i, carry):
            attend(i * C_ATT_ROWS, C_ATT_ROWS, None, lambda kvh: vtw_ref[kvh, i])
            return carry
        lax.fori_loop(0, n_far_wide, far_wide_body, 0)
        j_left = n_far_wide * (C_ATT_ROWS // blk)
        n_left = n_far - j_left

        @pl.when(n_left >= 2)
        def _():
            attend(j_left * blk, 2 * blk, None, two_blocks(j_left))

        @pl.when(jnp.bitwise_and(n_left, 1) == 1)
        def _():
            j = j_left + n_left - 1
            attend(j * blk, blk, None, one_block(j))
        after_far()

        @pl.when(qb >= 1)
        def _():
            attend((qb - 1) * blk, 2 * blk,
                   lambda h: jnp.concatenate([bias_ref[h, 1], bias_ref[h, 0]], axis=0), two_blocks(qb - 1))

        @pl.when(qb == 0)
        def _():
            attend(0, blk, lambda h: bias_ref[h, 0], one_block(0))

    @pl.when(direct_ok)
    def _():
        run_attention(attend_direct, apply_far_bias)

    @pl.when(jnp.logical_not(direct_ok))
    def _():
        m_ref[...] = jnp.full(m_ref.shape, MASK_NEG, F32)
        run_attention(attend_online, lambda: None)

    for kvh in range(C_KV_HEADS):
        o_t = acc_ref[kvh, 0:C_HEAD_DIM, :] / acc_ref[kvh, C_HEAD_DIM:C_HEAD_DIM + 1, :]
        for g in range(C_GROUP):
            h = kvh * C_GROUP + g
            o_ref[:, h * C_HEAD_DIM:(h + 1) * C_HEAD_DIM] = o_t[:, g * blk:(g + 1) * blk].T.astype(o_ref.dtype)


def _dsa(p_main, p_extra, bias_tiles, rel_bias, bsz, seq):
    assert seq % C_SEL_ROWS == 0 and seq % C_ATT_ROWS == 0
    n = p_main.shape[0]
    nb = seq // C_QBLOCK
    blk = C_QBLOCK
    ksel = min(C_TOPK_MAX, seq // 4)
    n_pair = C_IDX_HEADS // 2
    d_aug = C_HEAD_DIM + C_VPAD
    head_sel = jnp.repeat(jnp.eye(C_HEADS, LANES, dtype=BF16), C_HEAD_DIM, axis=0)
    row = lambda b, i: b * nb + i
    return pl.pallas_call(
        functools.partial(_dsa_kernel, seq=seq, ksel=ksel),
        grid=(bsz, nb),
        in_specs=[
            pl.BlockSpec(memory_space=pltpu.SMEM),
            pl.BlockSpec((blk, C_WIDTH), lambda b, i: (row(b, i), 0)),
            pl.BlockSpec((seq, 2 * C_KVW), lambda b, i: (b, C_WIDTH // (2 * C_KVW))),
            pl.BlockSpec((blk, C_IDX_HEADS * C_IDX_DIM),
                         lambda b, i: (row(b, i), (C_WIDTH + 2 * C_KVW) // (C_IDX_HEADS * C_IDX_DIM))),
            pl.BlockSpec((seq, LANES), lambda b, i: (b, 0)),
            pl.BlockSpec((C_HEADS, 2, blk, blk), lambda b, i: (0, 0, 0, 0)),
            pl.BlockSpec((C_WIDTH, LANES), lambda b, i: (0, 0)),
        ],
        out_specs=pl.BlockSpec((blk, C_WIDTH), lambda b, i: (row(b, i), 0)),
        out_shape=jax.ShapeDtypeStruct((n, C_WIDTH), BF16),
        scratch_shapes=[
            pltpu.VMEM((seq, blk), F32),
            pltpu.VMEM((seq, blk), F32),
            pltpu.VMEM((C_KV_HEADS, nb, d_aug, blk), BF16),
            pltpu.VMEM((C_KV_HEADS, seq // C_ATT_ROWS, d_aug, C_ATT_ROWS), BF16),
            pltpu.VMEM((seq, LANES), BF16),
            pltpu.VMEM((seq, LANES), BF16),
            pltpu.VMEM((n_pair * blk, LANES), BF16),
            pltpu.VMEM((C_KV_HEADS, C_GROUP * blk, C_HEAD_DIM), BF16),
            pltpu.VMEM((C_KV_HEADS, d_aug, C_GROUP * blk), F32),
            pltpu.VMEM((C_KV_HEADS, 1, C_GROUP * blk), F32),
            pltpu.VMEM((SUBLANES, LANES), F32),
        ],
        compiler_params=_params("parallel", "arbitrary"),
        name="dsa",
    )(rel_bias, p_main, p_main, p_main, p_extra, bias_tiles, head_sel)


AB_MAIN = 2 * A_WIDTH + 2 * B_HEADS * B_DK + 2 * B_WIDTH
C_MAIN = C_WIDTH + 2 * C_KVW + C_IDX_HEADS * C_IDX_DIM


def _extra_cols(w_in_all, n_main):
    n_extra = w_in_all.shape[2] - n_main
    return jnp.pad(w_in_all[:, :, n_main:], ((0, 0), (0, 0), (0, LANES - n_extra))).astype(BF16)


def _even_mixer(h, g, w_in, w_extra, li, v_ln_g, w_s, b_s, gate_w2, gate_b, out_norm_g, w_out, bsz, seq):
    p_main, p_extra = _norm_matmul(h, g, w_in, w_extra, li, AB_MAIN, F32)
    a_out = _sgu(p_main, v_ln_g, w_s, b_s.T)
    gate_w2_pad = jnp.pad(gate_w2, ((0, LANES - B_RANK), (0, 0))).astype(BF16)
    b_out = _gla(p_main, p_extra, gate_w2_pad, gate_b, out_norm_g, bsz, seq)
    return _proj_residual(h, [a_out, b_out], w_out, li)


def _odd_mixer(h, g, w_in, w_extra, li, w_out, bias_tiles, rel_bias, bsz, seq):
    p_main, p_extra = _norm_matmul(h, g, w_in, w_extra, li, C_MAIN, BF16)
    o = _dsa(p_main, p_extra, bias_tiles, rel_bias, bsz, seq)
    return _proj_residual(h, [o], w_out, li)


def _scaled_c_in(c_w_in):
    q_scale = (C_HEAD_DIM ** -0.5) * math.log2(math.e)
    col_scale = jnp.where(jnp.arange(c_w_in.shape[2]) < C_WIDTH, q_scale, 1.0).astype(F32)
    return (c_w_in * col_scale).astype(BF16)


def kernel(x, norm_mix_g, norm_ffn_g, final_norm_g, ab_w_in, a_v_ln_g, a_w_s, a_b_s, b_gate_w2,
           b_gate_b, b_out_norm_g, ab_w_out, c_w_in, c_w_out, rel_bias, ffn_w1, ffn_w2):
    bsz, seq, d = x.shape
    depth = norm_mix_g.shape[0]
    h = x.reshape(bsz * seq, d)
    bias_tiles = _bias_tiles(rel_bias)
    ab_in, ab_in_extra, ab_out = ab_w_in.astype(BF16), _extra_cols(ab_w_in, AB_MAIN), ab_w_out.astype(BF16)
    c_in, c_in_extra, c_out = _scaled_c_in(c_w_in), _extra_cols(c_w_in, C_MAIN), c_w_out.astype(BF16)
    w1, w2 = ffn_w1.astype(BF16), ffn_w2.astype(BF16)
    for layer in range(depth):
        i = layer // 2
        if layer % 2 == 0:
            h = _even_mixer(h, norm_mix_g[layer], ab_in, ab_in_extra, i, a_v_ln_g[i], a_w_s[i], a_b_s[i],
                            b_gate_w2[i], b_gate_b[i], b_out_norm_g[i], ab_out, bsz, seq)
        else:
            h = _odd_mixer(h, norm_mix_g[layer], c_in, c_in_extra, i, c_out, bias_tiles, rel_bias, bsz, seq)
        h = _ffn(h, norm_ffn_g[layer], w1, w2, layer, final_norm_g, layer == depth - 1)
    return h.reshape(bsz, seq, d)
```
